```python
import math
import jax, jax.numpy as jnp
from jax import lax
import numpy as np

D_MODEL = 1024
BATCH = 1
SEQ = 16384
DEPTH = 2
DEC_BATCH = 32
DEC_SEQ = 4
PAST_LEN = 16384
PAGE_SIZE = 128

N_MIXERS = 2
N_A = (DEPTH + 1) // 2
N_B = DEPTH // 2
HEAD_DIM_A = 64
N_HEADS_A = D_MODEL // (2 * HEAD_DIM_A)
V_DIM_A = 2 * HEAD_DIM_A
HEAD_DIM_B = 64
N_HEADS_B = D_MODEL // HEAD_DIM_B
MOBA_BLOCK = 256
MOBA_TOPK = 3
ROPE_THETA = 10000.0
D_FF = 2816
N_EXPERTS = 8
TOP_K = 2
D_FF_EXPERT = 3584
PLE_DIM = 256
Q_BLOCK = 128
EPS = 1e-6
LAMBDA_STD = 0.1

kernel_name = 'diffattn_moba_hybrid_step'


def rmsnorm(x, g):
    xf = x.astype(jnp.float32)
    y = xf * lax.rsqrt(jnp.mean(xf * xf, axis=-1, keepdims=True) + EPS)
    return (y * g.astype(jnp.float32)).astype(x.dtype)


def rope(x, pos):
    d = x.shape[-1]
    inv = ROPE_THETA ** (-jnp.arange(0, d, 2, dtype=jnp.float32) / d)
    ang = pos.astype(jnp.float32)[:, None] * inv[None, :]
    ang = jnp.concatenate([ang, ang], axis=-1)
    shp = (1, pos.shape[0]) + (1,) * (x.ndim - 3) + (d,)
    cos, sin = jnp.cos(ang).reshape(shp), jnp.sin(ang).reshape(shp)
    xf = x.astype(jnp.float32)
    rot = jnp.concatenate([-xf[..., d // 2:], xf[..., :d // 2]], axis=-1)
    return (xf * cos + rot * sin).astype(x.dtype)


def diff_qkv(h, w_qkv, g_q, g_k, pos):
    B, T, _ = h.shape
    q, k, v = jnp.split(h @ w_qkv, 3, axis=-1)
    q = rope(rmsnorm(q.reshape(B, T, N_HEADS_A, 2, HEAD_DIM_A), g_q), pos)
    k = rope(rmsnorm(k.reshape(B, T, N_HEADS_A, 2, HEAD_DIM_A), g_k), pos)
    return q, k, v.reshape(B, T, N_HEADS_A, V_DIM_A)


def diff_lambda(lam, lambda_init):
    lf = lam.astype(jnp.float32)
    return jnp.exp(jnp.sum(lf[0] * lf[1])) - jnp.exp(jnp.sum(lf[2] * lf[3])) + lambda_init


def diff_weights(s, lam):
    p = jax.nn.softmax(s, axis=-1)
    return p[:, :, 0] - lam * p[:, :, 1]


def diff_out(o, g_sub, lambda_init, w_o):
    B, T = o.shape[:2]
    o = rmsnorm(o, g_sub) * (1.0 - lambda_init)
    return o.reshape(B, T, -1) @ w_o


def diff_attn_prompt(q, k, v, lam):
    B, T = q.shape[:2]
    nq = T // Q_BLOCK
    scale = HEAD_DIM_A ** -0.5
    qb = q.reshape(B, nq, Q_BLOCK, N_HEADS_A, 2, HEAD_DIM_A).transpose(1, 0, 2, 3, 4, 5)
    kpos = jnp.arange(T)

    def block(args):
        qi, i = args
        qpos = i * Q_BLOCK + jnp.arange(Q_BLOCK)
        s = jnp.einsum('bqhcd,bkhcd->bhcqk', qi, k).astype(jnp.float32) * scale
        s = jnp.where(kpos[None, :] <= qpos[:, None], s, -jnp.inf)
        w = diff_weights(s, lam).astype(v.dtype)
        return jnp.einsum('bhqk,bkhe->bqhe', w, v)

    o = lax.map(block, (qb, jnp.arange(nq)))
    return o.transpose(1, 0, 2, 3, 4).reshape(B, T, N_HEADS_A, V_DIM_A)


def diff_attn_sample(q, k_new, v_new, pool_k, pool_v, j, page_table, lam):
    DB, T = q.shape[:2]
    past_len = page_table.shape[1] * PAGE_SIZE
    scale = HEAD_DIM_A ** -0.5
    k_past = pool_k[j, page_table].reshape(DB, past_len, N_HEADS_A, 2, HEAD_DIM_A)
    v_past = pool_v[j, page_table].reshape(DB, past_len, N_HEADS_A, V_DIM_A)
    s_past = jnp.einsum('bqhcd,bkhcd->bhcqk', q, k_past).astype(jnp.float32) * scale
    s_new = jnp.einsum('bqhcd,bkhcd->bhcqk', q, k_new).astype(jnp.float32) * scale
    s_new = jnp.where(jnp.tril(jnp.ones((T, T), dtype=bool)), s_new, -jnp.inf)
    w = diff_weights(jnp.concatenate([s_past, s_new], axis=-1), lam).astype(v_new.dtype)
    return (jnp.einsum('bhqk,bkhe->bqhe', w[..., :past_len], v_past)
            + jnp.einsum('bhqk,bkhe->bqhe', w[..., past_len:], v_new))


def moba_qkv(h, w_qkv, g_q, g_k, pos):
    B, T, _ = h.shape
    q, k, v = jnp.split(h @ w_qkv, 3, axis=-1)
    shp = (B, T, N_HEADS_B, HEAD_DIM_B)
    q = rope(rmsnorm(q.reshape(shp), g_q), pos)
    k = rope(rmsnorm(k.reshape(shp), g_k), pos)
    return q, k, v.reshape(shp)


def moba_prompt(q, k, v):
    B, T, H, d = q.shape
    nq = T // Q_BLOCK
    nblk = -(-T // MOBA_BLOCK)
    pad = nblk * MOBA_BLOCK - T
    n_sel = min(MOBA_TOPK, nblk)
    scale = d ** -0.5
    padw = ((0, 0), (0, pad), (0, 0), (0, 0))
    kb = jnp.pad(k, padw).reshape(B, nblk, MOBA_BLOCK, H, d).transpose(0, 3, 1, 2, 4)
    vb = jnp.pad(v, padw).reshape(B, nblk, MOBA_BLOCK, H, d).transpose(0, 3, 1, 2, 4)
    kmean = jnp.mean(kb.astype(jnp.float32), axis=3)
    qb = q.reshape(B, nq, Q_BLOCK, H, d).transpose(1, 0, 3, 2, 4)
    bi = jnp.arange(B)[:, None, None, None]
    hi = jnp.arange(H)[None, :, None, None]
    blk_ids = jnp.arange(nblk)
    own_off = jnp.arange(MOBA_BLOCK)
    n_k = n_sel * MOBA_BLOCK

    def block(args):
        qi, i = args
        start = i * Q_BLOCK
        cur = start // MOBA_BLOCK
        qpos = start + jnp.arange(Q_BLOCK)
        g = jnp.einsum('bhqd,bhnd->bhqn', qi.astype(jnp.float32), kmean)
        g = jnp.where(blk_ids < cur, g, -jnp.inf)
        _, idx = lax.top_k(g, n_sel)
        valid = idx < cur
        k_sel = kb[bi, hi, idx]
        v_sel = vb[bi, hi, idx]
        s_sel = jnp.einsum('bhqd,bhqnjd->bhqnj', qi, k_sel).astype(jnp.float32) * scale
        s_sel = jnp.where(valid[..., None], s_sel, -jnp.inf).reshape(B, H, Q_BLOCK, n_k)
        k_own = lax.dynamic_index_in_dim(kb, cur, axis=2, keepdims=False)
        v_own = lax.dynamic_index_in_dim(vb, cur, axis=2, keepdims=False)
        s_own = jnp.einsum('bhqd,bhjd->bhqj', qi, k_own).astype(jnp.float32) * scale
        s_own = jnp.where(cur * MOBA_BLOCK + own_off[None, :] <= qpos[:, None], s_own, -jnp.inf)
        p = jax.nn.softmax(jnp.concatenate([s_sel, s_own], axis=-1), axis=-1).astype(v.dtype)
        p_sel = p[..., :n_k].reshape(B, H, Q_BLOCK, n_sel, MOBA_BLOCK)
        return (jnp.einsum('bhqnj,bhqnjd->bhqd', p_sel, v_sel)
                + jnp.einsum('bhqj,bhjd->bhqd', p[..., n_k:], v_own))

    o = lax.map(block, (qb, jnp.arange(nq)))
    return o.transpose(1, 0, 3, 2, 4).reshape(B, T, H, d)


def moba_sample(q, k_new, v_new, pool_k, pool_v, j, page_table):
    DB, T, H, d = q.shape
    past_len = page_table.shape[1] * PAGE_SIZE
    n_full = past_len // MOBA_BLOCK
    own_start = n_full * MOBA_BLOCK
    own_len = past_len - own_start
    ppb = MOBA_BLOCK // PAGE_SIZE
    scale = d ** -0.5
    qh = q.transpose(0, 2, 1, 3)
    k_past = pool_k[j, page_table].reshape(DB, past_len, H, d)
    s_parts = []
    if n_full > 0:
        kmean = jnp.mean(k_past[:, :own_start].reshape(DB, n_full, MOBA_BLOCK, H, d).astype(jnp.float32), axis=2)
        g = jnp.einsum('bhtd,bnhd->bhtn', qh.astype(jnp.float32), kmean)
        _, idx = lax.top_k(g, min(MOBA_TOPK, n_full))
        logical = idx[..., None] * ppb + jnp.arange(ppb)
        phys = page_table[jnp.arange(DB)[:, None, None, None, None], logical]
        hi = jnp.arange(H)[None, :, None, None, None, None]
        rows = jnp.arange(PAGE_SIZE)
        k_sel = pool_k[j, phys[..., None], rows, hi].reshape(DB, H, T, -1, d)
        v_sel = pool_v[j, phys[..., None], rows, hi].reshape(DB, H, T, -1, d)
        s_parts.append(jnp.einsum('bhtd,bhtjd->bhtj', qh, k_sel).astype(jnp.float32) * scale)
    if own_len > 0:
        k_own = k_past[:, own_start:]
        v_own = pool_v[j, page_table[:, own_start // PAGE_SIZE:]].reshape(DB, own_len, H, d)
        s_parts.append(jnp.einsum('bhtd,bjhd->bhtj', qh, k_own).astype(jnp.float32) * scale)
    s_new = jnp.einsum('bhtd,bshd->bhts', qh, k_new).astype(jnp.float32) * scale
    s_parts.append(jnp.where(jnp.tril(jnp.ones((T, T), dtype=bool)), s_new, -jnp.inf))
    p = jax.nn.softmax(jnp.concatenate(s_parts, axis=-1), axis=-1).astype(v_new.dtype)
    o = jnp.einsum('bhts,bshd->bhtd', p[..., p.shape[-1] - T:], v_new)
    off = 0
    if n_full > 0:
        n = k_sel.shape[3]
        o = o + jnp.einsum('bhtj,bhtjd->bhtd', p[..., :n], v_sel)
        off = n
    if own_len > 0:
        o = o + jnp.einsum('bhtj,bjhd->bhtd', p[..., off:off + own_len], v_own)
    return o.transpose(0, 2, 1, 3)


def swiglu(h, wg, wu, wd):
    return (jax.nn.silu(h @ wg) * (h @ wu)) @ wd


def moe(h, w_router, wg, wu, wd):
    shp = h.shape
    hf = h.reshape(-1, shp[-1])
    logits = (hf @ w_router).astype(jnp.float32)
    top_v, top_i = lax.top_k(logits, TOP_K)
    gates = jax.nn.softmax(top_v, axis=-1)
    comb = jnp.sum(jax.nn.one_hot(top_i, N_EXPERTS, dtype=jnp.float32) * gates[..., None], axis=1)
    out = jnp.zeros_like(hf)
    for e in range(N_EXPERTS):
        out = out + comb[:, e:e + 1].astype(h.dtype) * swiglu(hf, wg[e], wu[e], wd[e])
    return out.reshape(shp)


def ple(h, p_i, g, w_gate, w_in):
    gate = jax.nn.sigmoid(rmsnorm(h, g) @ w_gate)
    return h + gate * (p_i @ w_in)


def setup_inputs(seed: int = 0) -> dict:
    key = jax.random.key(seed)
    keys = iter(jax.random.split(key, 40))
    f32 = jnp.float32
    n_pages = PAST_LEN // PAGE_SIZE
    n_used = DEC_BATCH * n_pages
    n_pool = n_used + n_used // 4

    def w(shape, fan_in):
        return jax.random.normal(next(keys), shape, f32) * fan_in ** -0.5

    def gain(shape):
        return 1.0 + 0.05 * jax.random.normal(next(keys), shape, f32)

    def normal(shape, std=1.0):
        return std * jax.random.normal(next(keys), shape, f32)

    D = D_MODEL
    x_prompt = normal((BATCH, SEQ, D))
    x_sample = normal((DEC_BATCH, DEC_SEQ, D))
    cache_k_a = normal((N_A, n_pool, PAGE_SIZE, N_HEADS_A, 2, HEAD_DIM_A))
    cache_v_a = normal((N_A, n_pool, PAGE_SIZE, N_HEADS_A, V_DIM_A))
    cache_k_b = normal((N_B, n_pool, PAGE_SIZE, N_HEADS_B, HEAD_DIM_B))
    cache_v_b = normal((N_B, n_pool, PAGE_SIZE, N_HEADS_B, HEAD_DIM_B))
    page_table = jax.random.permutation(next(keys), n_pool)[:n_used].reshape(DEC_BATCH, n_pages).astype(jnp.int32)
    p_prompt = normal((DEPTH, BATCH, SEQ, PLE_DIM))
    p_sample = normal((DEPTH, DEC_BATCH, DEC_SEQ, PLE_DIM))
    return {
        'x_prompt': x_prompt, 'x_sample': x_sample,
        'cache_k_a': cache_k_a, 'cache_v_a': cache_v_a,
        'cache_k_b': cache_k_b, 'cache_v_b': cache_v_b,
        'page_table': page_table, 'p_prompt': p_prompt, 'p_sample': p_sample,
        'g_mix': gain((DEPTH, D)), 'g_ffn': gain((DEPTH, D)), 'g_ple': gain((DEPTH, D)),
        'w_ple_gate': w((DEPTH, D, D), D), 'w_ple_in': w((DEPTH, PLE_DIM, D), PLE_DIM),
        'w_qkv_a': w((N_A, D, 3 * D), D), 'g_q_a': gain((N_A, HEAD_DIM_A)), 'g_k_a': gain((N_A, HEAD_DIM_A)),
        'lam_a': normal((N_A, 4, HEAD_DIM_A), LAMBDA_STD), 'g_sub_a': gain((N_A, V_DIM_A)),
        'w_o_a': w((N_A, D, D), D),
        'w_qkv_b': w((N_B, D, 3 * D), D), 'g_q_b': gain((N_B, HEAD_DIM_B)), 'g_k_b': gain((N_B, HEAD_DIM_B)),
        'w_o_b': w((N_B, D, D), D),
        'w_ff_gate': w((N_A, D, D_FF), D), 'w_ff_up': w((N_A, D, D_FF), D), 'w_ff_down': w((N_A, D_FF, D), D_FF),
        'w_router': w((N_B, D, N_EXPERTS), D),
        'w_ex_gate': w((N_B, N_EXPERTS, D, D_FF_EXPERT), D), 'w_ex_up': w((N_B, N_EXPERTS, D, D_FF_EXPERT), D),
        'w_ex_down': w((N_B, N_EXPERTS, D_FF_EXPERT, D), D_FF_EXPERT),
    }


def reference(x_prompt, x_sample, cache_k_a, cache_v_a, cache_k_b, cache_v_b, page_table, p_prompt, p_sample,
              g_mix, g_ffn, g_ple, w_ple_gate, w_ple_in, w_qkv_a, g_q_a, g_k_a, lam_a, g_sub_a, w_o_a,
              w_qkv_b, g_q_b, g_k_b, w_o_b, w_ff_gate, w_ff_up, w_ff_down, w_router, w_ex_gate, w_ex_up, w_ex_down):
    past_len = page_table.shape[1] * PAGE_SIZE
    pos_p = jnp.arange(x_prompt.shape[1])
    pos_s = past_len + jnp.arange(x_sample.shape[1])
    xp, xs = x_prompt, x_sample
    ka_p, va_p, kb_p, vb_p = [], [], [], []
    ka_s, va_s, kb_s, vb_s = [], [], [], []
    for i in range(DEPTH):
        j = i // N_MIXERS
        hp, hs = rmsnorm(xp, g_mix[i]), rmsnorm(xs, g_mix[i])
        if i % N_MIXERS == 0:
            lam_init = 0.8 - 0.6 * math.exp(-0.3 * i)
            lam = diff_lambda(lam_a[j], lam_init)
            qp, kp, vp = diff_qkv(hp, w_qkv_a[j], g_q_a[j], g_k_a[j], pos_p)
            qs, ks, vs = diff_qkv(hs, w_qkv_a[j], g_q_a[j], g_k_a[j], pos_s)
            op = diff_attn_prompt(qp, kp, vp, lam)
            os_ = diff_attn_sample(qs, ks, vs, cache_k_a, cache_v_a, j, page_table, lam)
            xp = xp + diff_out(op, g_sub_a[j], lam_init, w_o_a[j])
            xs = xs + diff_out(os_, g_sub_a[j], lam_init, w_o_a[j])
            ka_p.append(kp); va_p.append(vp); ka_s.append(ks); va_s.append(vs)
            hp, hs = rmsnorm(xp, g_ffn[i]), rmsnorm(xs, g_ffn[i])
            xp = xp + swiglu(hp, w_ff_gate[j], w_ff_up[j], w_ff_down[j])
            xs = xs + swiglu(hs, w_ff_gate[j], w_ff_up[j], w_ff_down[j])
        else:
            qp, kp, vp = moba_qkv(hp, w_qkv_b[j], g_q_b[j], g_k_b[j], pos_p)
            qs, ks, vs = moba_qkv(hs, w_qkv_b[j], g_q_b[j], g_k_b[j], pos_s)
            op = moba_prompt(qp, kp, vp)
            os_ = moba_sample(qs, ks, vs, cache_k_b, cache_v_b, j, page_table)
            xp = xp + op.reshape(xp.shape) @ w_o_b[j]
            xs = xs + os_.reshape(xs.shape) @ w_o_b[j]
            kb_p.append(kp); vb_p.append(vp); kb_s.append(ks); vb_s.append(vs)
            hp, hs = rmsnorm(xp, g_ffn[i]), rmsnorm(xs, g_ffn[i])
            xp = xp + moe(hp, w_router[j], w_ex_gate[j], w_ex_up[j], w_ex_down[j])
            xs = xs + moe(hs, w_router[j], w_ex_gate[j], w_ex_up[j], w_ex_down[j])
        xp = ple(xp, p_prompt[i], g_ple[i], w_ple_gate[i], w_ple_in[i])
        xs = ple(xs, p_sample[i], g_ple[i], w_ple_gate[i], w_ple_in[i])
    return (xp, xs, jnp.stack(ka_p), jnp.stack(va_p), jnp.stack(kb_p), jnp.stack(vb_p),
            jnp.stack(ka_s), jnp.stack(va_s), jnp.stack(kb_s), jnp.stack(vb_s))
```

```python
import functools
import math

import jax
import jax.numpy as jnp
from jax import lax
from jax.experimental import pallas as pl
from jax.experimental.pallas import tpu as pltpu

F32 = jnp.float32
BF16 = jnp.bfloat16
HIGHEST = lax.Precision.HIGHEST

D_MODEL = 1024
HEAD_DIM = 64
N_HEADS_A = 8
N_HEADS_B = 16
PAGE_SIZE = 128
MOBA_BLOCK = 256
MOBA_TOPK = 3
ROPE_THETA = 10000.0
N_EXPERTS = 8
EPS = 1e-6
LANES = 128
MASK_NEG = -1e30
KNOCKED = -3e38
SEL_BIAS = float(2 ** 30)
VMEM_LIMIT = 56 * 1024 * 1024

NT_DIMS = (((1,), (1,)), ((), ()))


def _cparams(*sem):
    return pltpu.CompilerParams(dimension_semantics=sem, vmem_limit_bytes=VMEM_LIMIT)


def _rms_rows(x, g):
    ms = jnp.mean(x * x, axis=-1, keepdims=True)
    return x * lax.rsqrt(ms + EPS) * g


def _sigmoid(x):
    return 1.0 / (1.0 + jnp.exp(-x))


def _qkv_body(x_ref, g_ref, wqk_ref, wv_ref, gq_ref, gk_ref, cos_ref, sin_ref,
              qT_ref, kT_ref, kTb_ref, v_ref, vb_ref, *rest, with_kmean):
    if with_kmean:
        kmean_ref, y_scr = rest
    else:
        (y_scr,) = rest
    hn = _rms_rows(x_ref[...], g_ref[...]).astype(BF16)
    v = jnp.dot(hn, wv_ref[...], preferred_element_type=F32)
    v_ref[...] = v
    vb_ref[...] = v.astype(BF16)
    y_scr[...] = lax.dot_general(wqk_ref[...], hn, NT_DIMS, preferred_element_type=F32)
    cos = cos_ref[...]
    sin = sin_ref[...]
    half = HEAD_DIM // 2
    n_groups = D_MODEL // HEAD_DIM
    for grp in range(2 * n_groups):
        y = y_scr[grp * HEAD_DIM:(grp + 1) * HEAD_DIM, :]
        gain = gq_ref[...] if grp < n_groups else gk_ref[...]
        ms = jnp.mean(y * y, axis=0, keepdims=True)
        y = y * lax.rsqrt(ms + EPS) * gain
        rot = jnp.concatenate([-y[half:, :], y[:half, :]], axis=0)
        y = y * cos + rot * sin
        if grp < n_groups:
            qT_ref[grp * HEAD_DIM:(grp + 1) * HEAD_DIM, :] = y
        else:
            r = (grp - n_groups) * HEAD_DIM
            kT_ref[r:r + HEAD_DIM, :] = y
            kTb_ref[r:r + HEAD_DIM, :] = y.astype(BF16)
    if with_kmean:
        t = x_ref.shape[0]
        avg = jnp.full((8, t), 1.0 / t, F32)
        km = lax.dot_general(avg, kT_ref[...], NT_DIMS, precision=HIGHEST,
                             preferred_element_type=F32)
        kmean_ref[0] = km[0:1, :]


def _qkv(x, g, wqkT, wv, gq, gk, cosT, sinT, *, tile, with_kmean):
    T = x.shape[0]
    assert T % tile == 0
    n = T // tile
    out_shape = [
        jax.ShapeDtypeStruct((D_MODEL, T), F32),
        jax.ShapeDtypeStruct((D_MODEL, T), F32),
        jax.ShapeDtypeStruct((D_MODEL, T), BF16),
        jax.ShapeDtypeStruct((T, D_MODEL), F32),
        jax.ShapeDtypeStruct((T, D_MODEL), BF16),
    ]
    colblk = pl.BlockSpec((D_MODEL, tile), lambda i: (0, i))
    rowblk = pl.BlockSpec((tile, D_MODEL), lambda i: (i, 0))
    out_specs = [colblk, colblk, colblk, rowblk, rowblk]
    if with_kmean:
        assert tile == MOBA_BLOCK
        out_shape.append(jax.ShapeDtypeStruct((n, 1, D_MODEL), F32))
        out_specs.append(pl.BlockSpec((1, 1, D_MODEL), lambda i: (i, 0, 0)))
    const2 = lambda i: (0, 0)
    return pl.pallas_call(
        functools.partial(_qkv_body, with_kmean=with_kmean),
        grid=(n,),
        in_specs=[
            rowblk,
            pl.BlockSpec((1, D_MODEL), const2),
            pl.BlockSpec((2 * D_MODEL, D_MODEL), const2),
            pl.BlockSpec((D_MODEL, D_MODEL), const2),
            pl.BlockSpec((HEAD_DIM, 1), const2),
            pl.BlockSpec((HEAD_DIM, 1), const2),
            pl.BlockSpec((HEAD_DIM, tile), lambda i: (0, i)),
            pl.BlockSpec((HEAD_DIM, tile), lambda i: (0, i)),
        ],
        out_specs=out_specs,
        out_shape=out_shape,
        scratch_shapes=[pltpu.VMEM((2 * D_MODEL, tile), F32)],
        compiler_params=_cparams("arbitrary"),
        name="qkv_norm_rope",
    )(x, g, wqkT, wv, gq, gk, cosT, sinT)


def _online_update(s, vv, m_ref, l_ref, acc_ref):
    m_old = m_ref[...]
    m_new = jnp.maximum(m_old, jnp.max(s, axis=1, keepdims=True))
    p = jnp.exp(s - m_new)
    alpha = jnp.exp(m_old - m_new)
    l_ref[...] = alpha * l_ref[...] + jnp.sum(p, axis=1, keepdims=True)
    acc_ref[...] = alpha * acc_ref[...] + jnp.dot(p.astype(BF16), vv, preferred_element_type=F32)
    m_ref[...] = m_new


def _init_softmax_state(m_ref, l_ref, acc_ref):
    m_ref[...] = jnp.full(m_ref.shape, MASK_NEG, F32)
    l_ref[...] = jnp.zeros(l_ref.shape, F32)
    acc_ref[...] = jnp.zeros(acc_ref.shape, F32)


def _diff_lambda(lam_ref, lam_init):
    lf = lam_ref[...]
    a = jnp.sum(lf[0:1, :] * lf[1:2, :], axis=1, keepdims=True)
    b = jnp.sum(lf[2:3, :] * lf[3:4, :], axis=1, keepdims=True)
    return jnp.exp(a) - jnp.exp(b) + lam_init


def _diff_attn_body(qT_ref, kT_ref, v_ref, lam_ref, gsub_ref, o_ref,
                    m1, l1, a1, m2, l2, a2, *, tq, tk, lam_init):
    qi = pl.program_id(1)
    q = qT_ref[...].T * (HEAD_DIM ** -0.5)
    lane = lax.broadcasted_iota(jnp.int32, q.shape, 1)
    q1 = jnp.where(lane < HEAD_DIM, q, 0.0).astype(BF16)
    q2 = jnp.where(lane >= HEAD_DIM, q, 0.0).astype(BF16)
    _init_softmax_state(m1, l1, a1)
    _init_softmax_state(m2, l2, a2)

    def step(kb, masked):
        off = pl.multiple_of(kb * tk, tk)
        kt = kT_ref[:, pl.ds(off, tk)]
        vv = v_ref[pl.ds(off, tk), :]
        s1 = jnp.dot(q1, kt, preferred_element_type=F32)
        s2 = jnp.dot(q2, kt, preferred_element_type=F32)
        if masked:
            row = qi * tq + lax.broadcasted_iota(jnp.int32, (tq, tk), 0)
            col = off + lax.broadcasted_iota(jnp.int32, (tq, tk), 1)
            keep = col <= row
            s1 = jnp.where(keep, s1, MASK_NEG)
            s2 = jnp.where(keep, s2, MASK_NEG)
        _online_update(s1, vv, m1, l1, a1)
        _online_update(s2, vv, m2, l2, a2)

    n_full = qi * (tq // tk)

    def loop_body(kb, c):
        step(kb, False)
        return c

    lax.fori_loop(0, n_full, loop_body, 0)
    for j in range(tq // tk):
        step(n_full + j, True)

    lam = _diff_lambda(lam_ref, lam_init)
    o = a1[...] / l1[...] - lam * (a2[...] / l2[...])
    o = _rms_rows(o, gsub_ref[...]) * (1.0 - lam_init)
    o_ref[...] = o.astype(o_ref.dtype)


def _diff_attn_prompt(qT, kTb, vb, lam, gsub, *, lam_init, tq, tk):
    T = qT.shape[1]
    hw = 2 * HEAD_DIM
    return pl.pallas_call(
        functools.partial(_diff_attn_body, tq=tq, tk=tk, lam_init=lam_init),
        grid=(N_HEADS_A, T // tq),
        in_specs=[
            pl.BlockSpec((hw, tq), lambda h, i: (h, i)),
            pl.BlockSpec((hw, T), lambda h, i: (h, 0)),
            pl.BlockSpec((T, hw), lambda h, i: (0, h)),
            pl.BlockSpec((4, HEAD_DIM), lambda h, i: (0, 0)),
            pl.BlockSpec((1, hw), lambda h, i: (0, 0)),
        ],
        out_specs=pl.BlockSpec((tq, hw), lambda h, i: (i, h)),
        out_shape=jax.ShapeDtypeStruct((T, D_MODEL), BF16),
        scratch_shapes=[
            pltpu.VMEM((tq, 1), F32), pltpu.VMEM((tq, 1), F32), pltpu.VMEM((tq, hw), F32),
            pltpu.VMEM((tq, 1), F32), pltpu.VMEM((tq, 1), F32), pltpu.VMEM((tq, hw), F32),
        ],
        compiler_params=_cparams("arbitrary", "arbitrary"),
        name="diff_attn_prompt",
    )(qT, kTb, vb, lam, gsub)


def _topk_pick(g, lanef, k):
    picked = jnp.zeros(g.shape, jnp.bool_)
    idxs = []
    for _ in range(k):
        m = jnp.max(g, axis=1, keepdims=True)
        idx = jnp.min(jnp.where(g == m, lanef, float(4 * LANES)), axis=1, keepdims=True)
        hit = lanef == idx
        picked = jnp.logical_or(picked, hit)
        g = jnp.where(hit, KNOCKED, g)
        idxs.append(idx)
    return picked, idxs


def _moba_body(qT_ref, kT_ref, v_ref, km_ref, o_ref, ma, la, aa, mb, lb, ab, *, tq):
    qi = pl.program_id(1)
    qf = qT_ref[...].T
    lane = lax.broadcasted_iota(jnp.int32, qf.shape, 1)
    lanef = lane.astype(F32)
    lo = lane < HEAD_DIM
    qs = qf * (HEAD_DIM ** -0.5)

    def select_bias(hh, base):
        g = jnp.dot(qf, km_ref[hh], precision=HIGHEST, preferred_element_type=F32)
        blk = lane - base
        past = jnp.logical_and(blk >= 0, blk < qi)
        g = jnp.where(past, g, MASK_NEG)
        picked, _ = _topk_pick(g, lanef, MOBA_TOPK)
        sel = jnp.logical_or(jnp.logical_and(picked, past), blk == qi)
        return jnp.where(sel, 0.0, -SEL_BIAS)

    qa = jnp.where(lo, qs, select_bias(0, HEAD_DIM)).astype(BF16)
    qb = jnp.where(lo, select_bias(1, 0), qs).astype(BF16)
    _init_softmax_state(ma, la, aa)
    _init_softmax_state(mb, lb, ab)
    sub = lax.broadcasted_iota(jnp.int32, (HEAD_DIM, tq), 0)
    vlane = lax.broadcasted_iota(jnp.int32, (tq, 2 * HEAD_DIM), 1)

    def step(kb, masked):
        off = pl.multiple_of(kb * tq, tq)
        kt = kT_ref[:, pl.ds(off, tq)]
        vv = v_ref[pl.ds(off, tq), :]
        onehot = jnp.where(sub == kb, 1.0, 0.0).astype(BF16)
        kta = jnp.concatenate([kt[:HEAD_DIM, :], onehot], axis=0)
        ktb = jnp.concatenate([onehot, kt[HEAD_DIM:, :]], axis=0)
        sa = jnp.dot(qa, kta, preferred_element_type=F32)
        sb = jnp.dot(qb, ktb, preferred_element_type=F32)
        if masked:
            row = lax.broadcasted_iota(jnp.int32, (tq, tq), 0)
            col = lax.broadcasted_iota(jnp.int32, (tq, tq), 1)
            keep = col <= row
            sa = jnp.where(keep, sa, MASK_NEG)
            sb = jnp.where(keep, sb, MASK_NEG)
        zero = jnp.zeros_like(vv)
        _online_update(sa, jnp.where(vlane < HEAD_DIM, vv, zero), ma, la, aa)
        _online_update(sb, jnp.where(vlane >= HEAD_DIM, vv, zero), mb, lb, ab)

    def loop_body(kb, c):
        step(kb, False)
        return c

    lax.fori_loop(0, qi, loop_body, 0)
    step(qi, True)
    o_ref[...] = (aa[...] / la[...] + ab[...] / lb[...]).astype(o_ref.dtype)


def _moba_prompt(qT, kTb, vb, km_aug):
    T = qT.shape[1]
    tq = MOBA_BLOCK
    hw = 2 * HEAD_DIM
    assert T % tq == 0 and T // tq <= HEAD_DIM
    return pl.pallas_call(
        functools.partial(_moba_body, tq=tq),
        grid=(N_HEADS_B // 2, T // tq),
        in_specs=[
            pl.BlockSpec((hw, tq), lambda h, i: (h, i)),
            pl.BlockSpec((hw, T), lambda h, i: (h, 0)),
            pl.BlockSpec((T, hw), lambda h, i: (0, h)),
            pl.BlockSpec((2, hw, hw), lambda h, i: (h, 0, 0)),
        ],
        out_specs=pl.BlockSpec((tq, hw), lambda h, i: (i, h)),
        out_shape=jax.ShapeDtypeStruct((T, D_MODEL), BF16),
        scratch_shapes=[
            pltpu.VMEM((tq, 1), F32), pltpu.VMEM((tq, 1), F32), pltpu.VMEM((tq, hw), F32),
            pltpu.VMEM((tq, 1), F32), pltpu.VMEM((tq, 1), F32), pltpu.VMEM((tq, hw), F32),
        ],
        compiler_params=_cparams("arbitrary", "arbitrary"),
        name="moba_attn_prompt",
    )(qT, kTb, vb, km_aug)


def _moba_kmean_aug(kmean):
    n_blk = kmean.shape[0]
    km = kmean.reshape(n_blk, N_HEADS_B, HEAD_DIM).transpose(1, 2, 0)
    km = jnp.pad(km, ((0, 0), (0, 0), (0, HEAD_DIM - n_blk)))
    z = jnp.zeros_like(km)
    even = jnp.concatenate([jnp.concatenate([z, km], axis=2), jnp.concatenate([z, z], axis=2)], axis=1)
    odd = jnp.concatenate([jnp.concatenate([z, z], axis=2), jnp.concatenate([km, z], axis=2)], axis=1)
    is_even = (jnp.arange(N_HEADS_B) % 2 == 0)[:, None, None]
    return jnp.where(is_even, even, odd)


def _oproj_body(x_ref, o_ref, w_ref, y_ref):
    y_ref[...] = x_ref[...] + jnp.dot(o_ref[...], w_ref[...], preferred_element_type=F32)


def _oproj(x, o, w, *, tile):
    T = x.shape[0]
    rowblk = pl.BlockSpec((tile, D_MODEL), lambda i: (i, 0))
    return pl.pallas_call(
        _oproj_body,
        grid=(T // tile,),
        in_specs=[rowblk, rowblk, pl.BlockSpec((D_MODEL, D_MODEL), lambda i: (0, 0))],
        out_specs=rowblk,
        out_shape=jax.ShapeDtypeStruct((T, D_MODEL), F32),
        compiler_params=_cparams("arbitrary"),
        name="attn_out_proj",
    )(x, o, w)


def _swiglu_chunk(hn, wg, wu, wd):
    a = jnp.dot(hn, wg, preferred_element_type=F32)
    u = jnp.dot(hn, wu, preferred_element_type=F32)
    act = (a * _sigmoid(a) * u).astype(BF16)
    return jnp.dot(act, wd, preferred_element_type=F32)


def _ffn_body(x_ref, g_ref, wg_ref, wu_ref, wd_ref, y_ref, hn_scr, acc_scr):
    f = pl.program_id(1)

    @pl.when(f == 0)
    def _():
        x = x_ref[...]
        hn_scr[...] = _rms_rows(x, g_ref[...]).astype(BF16)
        acc_scr[...] = x

    acc_scr[...] += _swiglu_chunk(hn_scr[...], wg_ref[...], wu_ref[...], wd_ref[...])

    @pl.when(f == pl.num_programs(1) - 1)
    def _():
        y_ref[...] = acc_scr[...]


def _ffn(x, g, wg, wu, wd, *, tile, tf):
    T = x.shape[0]
    F = wg.shape[1]
    assert F % tf == 0
    rowblk = pl.BlockSpec((tile, D_MODEL), lambda i, f: (i, 0))
    return pl.pallas_call(
        _ffn_body,
        grid=(T // tile, F // tf),
        in_specs=[
            rowblk,
            pl.BlockSpec((1, D_MODEL), lambda i, f: (0, 0)),
            pl.BlockSpec((D_MODEL, tf), lambda i, f: (0, f)),
            pl.BlockSpec((D_MODEL, tf), lambda i, f: (0, f)),
            pl.BlockSpec((tf, D_MODEL), lambda i, f: (f, 0)),
        ],
        out_specs=rowblk,
        out_shape=jax.ShapeDtypeStruct((T, D_MODEL), F32),
        scratch_shapes=[pltpu.VMEM((tile, D_MODEL), BF16), pltpu.VMEM((tile, D_MODEL), F32)],
        compiler_params=_cparams("arbitrary", "arbitrary"),
        name="dense_swiglu",
    )(x, g, wg, wu, wd)


def _router_body(x_ref, g_ref, wr_ref, hn_ref, comb_ref):
    hf = _rms_rows(x_ref[...], g_ref[...])
    hn_ref[...] = hf.astype(BF16)
    logits = jnp.dot(hf, wr_ref[...], precision=HIGHEST, preferred_element_type=F32)
    lane = lax.broadcasted_iota(jnp.int32, logits.shape, 1)
    lanef = lane.astype(F32)
    logits = jnp.where(lane < N_EXPERTS, logits, MASK_NEG)
    v1 = jnp.max(logits, axis=1, keepdims=True)
    i1 = jnp.min(jnp.where(logits == v1, lanef, float(LANES)), axis=1, keepdims=True)
    rest = jnp.where(lanef == i1, KNOCKED, logits)
    v2 = jnp.max(rest, axis=1, keepdims=True)
    i2 = jnp.min(jnp.where(rest == v2, lanef, float(LANES)), axis=1, keepdims=True)
    e2 = jnp.exp(v2 - v1)
    g1 = 1.0 / (1.0 + e2)
    g2 = e2 / (1.0 + e2)
    comb_ref[...] = jnp.where(lanef == i1, g1, 0.0) + jnp.where(lanef == i2, g2, 0.0)


def _router(x, g, wr, *, tile):
    T = x.shape[0]
    rowblk = pl.BlockSpec((tile, D_MODEL), lambda i: (i, 0))
    return pl.pallas_call(
        _router_body,
        grid=(T // tile,),
        in_specs=[rowblk, pl.BlockSpec((1, D_MODEL), lambda i: (0, 0)),
                  pl.BlockSpec((D_MODEL, LANES), lambda i: (0, 0))],
        out_specs=[rowblk, pl.BlockSpec((tile, LANES), lambda i: (i, 0))],
        out_shape=[jax.ShapeDtypeStruct((T, D_MODEL), BF16), jax.ShapeDtypeStruct((T, LANES), F32)],
        compiler_params=_cparams("arbitrary"),
        name="moe_router",
    )(x, g, wr)


def _experts_body(x_ref, hn_ref, comb_ref, wg_ref, wu_ref, wd_ref, y_ref, acc_scr):
    e = pl.program_id(1)
    f = pl.program_id(2)

    @pl.when(jnp.logical_and(e == 0, f == 0))
    def _():
        acc_scr[...] = x_ref[...]

    comb = comb_ref[...]
    lane = lax.broadcasted_iota(jnp.int32, comb.shape, 1)
    w_e = jnp.sum(jnp.where(lane == e, comb, 0.0), axis=1, keepdims=True)
    acc_scr[...] += w_e * _swiglu_chunk(hn_ref[...], wg_ref[...], wu_ref[...], wd_ref[...])

    @pl.when(jnp.logical_and(e == pl.num_programs(1) - 1, f == pl.num_programs(2) - 1))
    def _():
        y_ref[...] = acc_scr[...]


def _experts(x, hn, comb, wg, wu, wd, *, tile, tf):
    T = x.shape[0]
    E, _, F = wg.shape
    assert F % tf == 0
    rowblk = pl.BlockSpec((tile, D_MODEL), lambda i, e, f: (i, 0))
    return pl.pallas_call(
        _experts_body,
        grid=(T // tile, E, F // tf),
        in_specs=[
            rowblk, rowblk,
            pl.BlockSpec((tile, LANES), lambda i, e, f: (i, 0)),
            pl.BlockSpec((None, D_MODEL, tf), lambda i, e, f: (e, 0, f)),
            pl.BlockSpec((None, D_MODEL, tf), lambda i, e, f: (e, 0, f)),
            pl.BlockSpec((None, tf, D_MODEL), lambda i, e, f: (e, f, 0)),
        ],
        out_specs=rowblk,
        out_shape=jax.ShapeDtypeStruct((T, D_MODEL), F32),
        scratch_shapes=[pltpu.VMEM((tile, D_MODEL), F32)],
        compiler_params=_cparams("arbitrary", "arbitrary", "arbitrary"),
        name="moe_experts",
    )(x, hn, comb, wg, wu, wd)


def _ple_body(x_ref, g_ref, wgate_ref, p_ref, win_ref, y_ref):
    x = x_ref[...]
    hn = _rms_rows(x, g_ref[...]).astype(BF16)
    gate = _sigmoid(jnp.dot(hn, wgate_ref[...], preferred_element_type=F32))
    emb = jnp.dot(p_ref[...].astype(BF16), win_ref[...], preferred_element_type=F32)
    y_ref[...] = x + gate * emb


def _ple(x, g, wgate, p, win, *, tile):
    T = x.shape[0]
    pd = p.shape[1]
    rowblk = pl.BlockSpec((tile, D_MODEL), lambda i: (i, 0))
    return pl.pallas_call(
        _ple_body,
        grid=(T // tile,),
        in_specs=[
            rowblk,
            pl.BlockSpec((1, D_MODEL), lambda i: (0, 0)),
            pl.BlockSpec((D_MODEL, D_MODEL), lambda i: (0, 0)),
            pl.BlockSpec((tile, pd), lambda i: (i, 0)),
            pl.BlockSpec((pd, D_MODEL), lambda i: (0, 0)),
        ],
        out_specs=rowblk,
        out_shape=jax.ShapeDtypeStruct((T, D_MODEL), F32),
        compiler_params=_cparams("arbitrary"),
        name="per_layer_embedding",
    )(x, g, wgate, p, win)


PAGES_PER_STEP = 8


def _diff_decode_body(pt_ref, qbd_ref, knew_ref, vnew_ref, lam_ref, gsub_ref, *rest,
                      dec_seq, lam_init):
    kT_refs = rest[:PAGES_PER_STEP]
    v_refs = rest[PAGES_PER_STEP:2 * PAGES_PER_STEP]
    o_ref, m_ref, l_ref, acc_ref = rest[2 * PAGES_PER_STEP:]
    j = pl.program_id(1)
    n_rows = qbd_ref.shape[0]

    @pl.when(j == 0)
    def _():
        _init_softmax_state(m_ref, l_ref, acc_ref)

    qbd = qbd_ref[...]
    s = jnp.concatenate(
        [jnp.dot(qbd, r[...].astype(BF16), preferred_element_type=F32) for r in kT_refs], axis=1)
    m_old = m_ref[...]
    m_new = jnp.maximum(m_old, jnp.max(s, axis=1, keepdims=True))
    p = jnp.exp(s - m_new)
    alpha = jnp.exp(m_old - m_new)
    l_ref[...] = alpha * l_ref[...] + jnp.sum(p, axis=1, keepdims=True)
    acc = alpha * acc_ref[...]
    for i, r in enumerate(v_refs):
        vcat = jnp.concatenate(
            [r[pl.ds(h, PAGE_SIZE, stride=N_HEADS_A), :] for h in range(N_HEADS_A)], axis=1)
        acc += jnp.dot(p[:, i * PAGE_SIZE:(i + 1) * PAGE_SIZE].astype(BF16), vcat.astype(BF16),
                       preferred_element_type=F32)
    acc_ref[...] = acc
    m_ref[...] = m_new

    @pl.when(j == pl.num_programs(1) - 1)
    def _():
        sn = lax.dot_general(qbd, knew_ref[...], NT_DIMS, preferred_element_type=F32)
        row = lax.broadcasted_iota(jnp.int32, sn.shape, 0)
        col = lax.broadcasted_iota(jnp.int32, sn.shape, 1)
        half = n_rows // 2
        t_of_row = jnp.right_shift(jnp.bitwise_and(row, half - 1), 3)
        sn = jnp.where(jnp.logical_and(col <= t_of_row, col < dec_seq), sn, MASK_NEG)
        _online_update(sn, vnew_ref[...], m_ref, l_ref, acc_ref)
        o = acc_ref[...] / l_ref[...]
        lam = _diff_lambda(lam_ref, lam_init)
        od = o[:half, :] - lam * o[half:, :]
        r2 = lax.broadcasted_iota(jnp.int32, od.shape, 0)
        c2 = lax.broadcasted_iota(jnp.int32, od.shape, 1)
        own = jnp.right_shift(c2, 7) == jnp.bitwise_and(r2, N_HEADS_A - 1)
        od = jnp.where(own, od, 0.0)
        ms = jnp.sum(od * od, axis=1, keepdims=True) / (2 * HEAD_DIM)
        od = od * lax.rsqrt(ms + EPS) * gsub_ref[...] * (1.0 - lam_init)
        rows = [jnp.sum(od[t * N_HEADS_A:(t + 1) * N_HEADS_A, :], axis=0, keepdims=True)
                for t in range(dec_seq)]
        o_ref[...] = jnp.concatenate(rows, axis=0)


def _diff_decode(page_table, qbd, knew, vnew, lam, gsub_tiled, kT_pages, v_rows, *, dec_seq,
                 lam_init):
    nb, n_pages = page_table.shape
    assert n_pages % PAGES_PER_STEP == 0
    n_rows = qbd.shape[1]
    kv_rows = PAGE_SIZE * N_HEADS_A

    def k_spec(i):
        return pl.BlockSpec((None, D_MODEL, PAGE_SIZE),
                            lambda b, j, pt: (pt[b, j * PAGES_PER_STEP + i], 0, 0))

    def v_spec(i):
        return pl.BlockSpec((kv_rows, 2 * HEAD_DIM),
                            lambda b, j, pt: (pt[b, j * PAGES_PER_STEP + i], 0))

    per_seq = lambda b, j, pt: (b, 0, 0)
    const2 = lambda b, j, pt: (0, 0)
    grid_spec = pltpu.PrefetchScalarGridSpec(
        num_scalar_prefetch=1,
        grid=(nb, n_pages // PAGES_PER_STEP),
        in_specs=[
            pl.BlockSpec((None, n_rows, D_MODEL), per_seq),
            pl.BlockSpec((None, PAGE_SIZE, D_MODEL), per_seq),
            pl.BlockSpec((None, PAGE_SIZE, D_MODEL), per_seq),
            pl.BlockSpec((4, HEAD_DIM), const2),
            pl.BlockSpec((1, D_MODEL), const2),
        ] + [k_spec(i) for i in range(PAGES_PER_STEP)] + [v_spec(i) for i in range(PAGES_PER_STEP)],
        out_specs=pl.BlockSpec((None, dec_seq, D_MODEL), per_seq),
        scratch_shapes=[pltpu.VMEM((n_rows, 1), F32), pltpu.VMEM((n_rows, 1), F32),
                        pltpu.VMEM((n_rows, D_MODEL), F32)],
    )
    return pl.pallas_call(
        functools.partial(_diff_decode_body, dec_seq=dec_seq, lam_init=lam_init),
        grid_spec=grid_spec,
        out_shape=jax.ShapeDtypeStruct((nb, dec_seq, D_MODEL), F32),
        compiler_params=_cparams("arbitrary", "arbitrary"),
        name="diff_attn_decode",
    )(page_table, qbd, knew, vnew, lam, gsub_tiled,
      *([kT_pages] * PAGES_PER_STEP), *([v_rows] * PAGES_PER_STEP))


def _moba_gate_body(pt_ref, qbd_ref, *rest):
    kT_refs = rest[:PAGES_PER_STEP]
    idx_ref, km_scr = rest[PAGES_PER_STEP:]
    j = pl.program_id(1)
    pages_per_block = MOBA_BLOCK // PAGE_SIZE
    blocks_per_step = PAGES_PER_STEP // pages_per_block

    @pl.when(j == 0)
    def _():
        km_scr[...] = jnp.zeros(km_scr.shape, F32)

    lane = lax.broadcasted_iota(jnp.int32, km_scr.shape, 1)
    km = km_scr[...]
    for bb in range(blocks_per_step):
        tot = jnp.sum(kT_refs[bb * pages_per_block][...], axis=1, keepdims=True)
        for i in range(1, pages_per_block):
            tot = tot + jnp.sum(kT_refs[bb * pages_per_block + i][...], axis=1, keepdims=True)
        km = jnp.where(lane == j * blocks_per_step + bb, tot * (1.0 / MOBA_BLOCK), km)
    km_scr[...] = km

    @pl.when(j == pl.num_programs(1) - 1)
    def _():
        n_blk = pl.num_programs(1) * blocks_per_step
        g = jnp.dot(qbd_ref[...], km_scr[...], precision=HIGHEST, preferred_element_type=F32)
        glane = lax.broadcasted_iota(jnp.int32, g.shape, 1)
        glanef = glane.astype(F32)
        g = jnp.where(glane < n_blk, g, MASK_NEG)
        _, idxs = _topk_pick(g, glanef, MOBA_TOPK)
        out = jnp.zeros(g.shape, F32)
        for r, idx in enumerate(idxs):
            out = jnp.where(glane == r, idx, out)
        idx_ref[...] = out.astype(jnp.int32)


def _moba_gate(page_table, qbd, kT_pages):
    nb, n_pages = page_table.shape
    assert n_pages % PAGES_PER_STEP == 0 and n_pages * PAGE_SIZE // MOBA_BLOCK <= LANES
    n_rows = qbd.shape[1]

    def k_spec(i):
        return pl.BlockSpec((None, D_MODEL, PAGE_SIZE),
                            lambda b, j, pt: (pt[b, j * PAGES_PER_STEP + i], 0, 0))

    per_seq = lambda b, j, pt: (b, 0, 0)
    grid_spec = pltpu.PrefetchScalarGridSpec(
        num_scalar_prefetch=1,
        grid=(nb, n_pages // PAGES_PER_STEP),
        in_specs=[pl.BlockSpec((None, n_rows, D_MODEL), per_seq)]
        + [k_spec(i) for i in range(PAGES_PER_STEP)],
        out_specs=pl.BlockSpec((None, n_rows, LANES), per_seq),
        scratch_shapes=[pltpu.VMEM((D_MODEL, LANES), F32)],
    )
    return pl.pallas_call(
        _moba_gate_body,
        grid_spec=grid_spec,
        out_shape=jax.ShapeDtypeStruct((nb, n_rows, LANES), jnp.int32),
        compiler_params=_cparams("arbitrary", "arbitrary"),
        name="moba_decode_gate",
    )(page_table, qbd, *([kT_pages] * PAGES_PER_STEP))


def _moba_sel_body(phys_ref, q_ref, knew_ref, vnew_ref, *rest, dec_seq, n_sel):
    k_refs = rest[:dec_seq * n_sel]
    v_refs = rest[dec_seq * n_sel:2 * dec_seq * n_sel]
    (o_ref,) = rest[2 * dec_seq * n_sel:]
    q = q_ref[...]
    lane = lax.broadcasted_iota(jnp.int32, (1, PAGE_SIZE), 1)
    olane = lax.broadcasted_iota(jnp.int32, o_ref.shape, 1)
    out = jnp.zeros(o_ref.shape, F32)
    for t in range(dec_seq):
        qcol = q[:, t:t + 1]
        ks = [k_refs[t * n_sel + s][...] for s in range(n_sel)] + [knew_ref[...]]
        vs = [v_refs[t * n_sel + s][...] for s in range(n_sel)] + [vnew_ref[...]]
        scores = [jnp.sum(k * qcol, axis=0, keepdims=True) for k in ks]
        scores[-1] = jnp.where(lane <= t, scores[-1], MASK_NEG)
        m = scores[0].max(axis=1, keepdims=True)
        for sc in scores[1:]:
            m = jnp.maximum(m, sc.max(axis=1, keepdims=True))
        ps = [jnp.exp(sc - m) for sc in scores]
        l = ps[0].sum(axis=1, keepdims=True)
        for pp in ps[1:]:
            l = l + pp.sum(axis=1, keepdims=True)
        o = jnp.sum(vs[0] * ps[0], axis=1, keepdims=True)
        for vv, pp in zip(vs[1:], ps[1:]):
            o = o + jnp.sum(vv * pp, axis=1, keepdims=True)
        out = jnp.where(olane == t, o / l, out)
    o_ref[...] = out


def _moba_decode_attend(phys, qT, knewT, vnewT, kT_slices, vT_slices, *, dec_seq):
    nb, nh, n_slices = phys.shape
    n_sel = n_slices // dec_seq

    def sel_spec(s):
        return pl.BlockSpec((None, None, HEAD_DIM, PAGE_SIZE),
                            lambda b, h, ph: (ph[b, h, s], h, 0, 0))

    per_head = pl.BlockSpec((None, None, HEAD_DIM, PAGE_SIZE), lambda b, h, ph: (b, h, 0, 0))
    grid_spec = pltpu.PrefetchScalarGridSpec(
        num_scalar_prefetch=1,
        grid=(nb, nh),
        in_specs=[per_head, per_head, per_head]
        + [sel_spec(s) for s in range(n_slices)] + [sel_spec(s) for s in range(n_slices)],
        out_specs=per_head,
    )
    return pl.pallas_call(
        functools.partial(_moba_sel_body, dec_seq=dec_seq, n_sel=n_sel),
        grid_spec=grid_spec,
        out_shape=jax.ShapeDtypeStruct((nb, nh, HEAD_DIM, PAGE_SIZE), F32),
        compiler_params=_cparams("arbitrary", "arbitrary"),
        name="moba_decode_attend",
    )(phys, qT, knewT, vnewT, *([kT_slices] * n_slices), *([vT_slices] * n_slices))


def _rope_tables(pos):
    inv = ROPE_THETA ** (-jnp.arange(0, HEAD_DIM, 2, dtype=F32) / HEAD_DIM)
    ang = pos.astype(F32)[:, None] * inv[None, :]
    ang = jnp.concatenate([ang, ang], axis=-1)
    return jnp.cos(ang).T, jnp.sin(ang).T


def _col(v):
    return v.reshape(-1, 1)


def _row(v):
    return v.reshape(1, -1)


def _pad_lanes(a, n):
    return jnp.pad(a, [(0, 0)] * (a.ndim - 1) + [(0, n - a.shape[-1])])


def kernel(x_prompt, x_sample, cache_k_a, cache_v_a, cache_k_b, cache_v_b, page_table, p_prompt, p_sample,
           g_mix, g_ffn, g_ple, w_ple_gate, w_ple_in, w_qkv_a, g_q_a, g_k_a, lam_a, g_sub_a, w_o_a,
           w_qkv_b, g_q_b, g_k_b, w_o_b, w_ff_gate, w_ff_up, w_ff_down, w_router, w_ex_gate, w_ex_up, w_ex_down):
    _, T, D = x_prompt.shape
    nb, dec_seq, _ = x_sample.shape
    n_dec = nb * dec_seq
    n_pool = cache_k_a.shape[1]
    past_len = page_table.shape[1] * PAGE_SIZE
    assert D == D_MODEL and n_dec % 8 == 0 and dec_seq <= PAGE_SIZE
    pt = page_table.astype(jnp.int32)

    tile_p = 512 if T % 512 == 0 else MOBA_BLOCK
    xp = x_prompt.reshape(T, D)
    xs = x_sample.reshape(n_dec, D)
    cos_p, sin_p = _rope_tables(jnp.arange(T))
    pos_s = past_len + jnp.tile(jnp.arange(dec_seq), nb)
    cos_s, sin_s = _rope_tables(pos_s)

    lam_init = 0.8 - 0.6 * math.exp(-0.3 * 0)
    wqkT = w_qkv_a[0][:, :2 * D].T.astype(BF16)
    wv = w_qkv_a[0][:, 2 * D:].astype(BF16)
    wo = w_o_a[0].astype(BF16)
    qkv_args = (_row(g_mix[0]), wqkT, wv, _col(g_q_a[0]), _col(g_k_a[0]))
    qT_p, kT_p, kTb_p, v_p, vb_p = _qkv(xp, *qkv_args, cos_p, sin_p, tile=MOBA_BLOCK, with_kmean=False)
    qT_s, kT_s, _, v_s, _ = _qkv(xs, *qkv_args, cos_s, sin_s, tile=n_dec, with_kmean=False)
    ka_p = kT_p.reshape(1, 1, N_HEADS_A, 2, HEAD_DIM, T).transpose(0, 1, 5, 2, 3, 4)
    va_p = v_p.reshape(1, 1, T, N_HEADS_A, 2 * HEAD_DIM)
    ka_s = kT_s.T.reshape(1, nb, dec_seq, N_HEADS_A, 2, HEAD_DIM)
    va_s = v_s.reshape(1, nb, dec_seq, N_HEADS_A, 2 * HEAD_DIM)

    o_p = _diff_attn_prompt(qT_p, kTb_p, vb_p, lam_a[0], _row(g_sub_a[0]), lam_init=lam_init,
                            tq=MOBA_BLOCK, tk=MOBA_BLOCK)

    q5 = (qT_s.T * (HEAD_DIM ** -0.5)).reshape(nb, dec_seq, N_HEADS_A, 2, HEAD_DIM)
    eye_h = jnp.eye(N_HEADS_A, dtype=F32)
    eye_c = jnp.eye(2, dtype=F32)
    qbd = jnp.einsum('bthcd,hH,cC->bcthHCd', q5, eye_h, eye_c).reshape(nb, 2 * dec_seq * N_HEADS_A, D)
    qbd = qbd.astype(BF16)
    knew = jnp.pad(kT_s.T.reshape(nb, dec_seq, D), ((0, 0), (0, PAGE_SIZE - dec_seq), (0, 0))).astype(BF16)
    vnew = jnp.pad(v_s.reshape(nb, dec_seq, D), ((0, 0), (0, PAGE_SIZE - dec_seq), (0, 0))).astype(BF16)
    kT_pages_a = cache_k_a[0].transpose(0, 2, 3, 4, 1).reshape(n_pool, D, PAGE_SIZE)
    v_rows_a = cache_v_a[0].reshape(n_pool * PAGE_SIZE * N_HEADS_A, 2 * HEAD_DIM)
    o_s = _diff_decode(pt, qbd, knew, vnew, lam_a[0], _row(jnp.tile(g_sub_a[0], N_HEADS_A)),
                       kT_pages_a, v_rows_a, dec_seq=dec_seq, lam_init=lam_init)
    o_s = o_s.reshape(n_dec, D).astype(BF16)

    xp = _oproj(xp, o_p, wo, tile=tile_p)
    xs = _oproj(xs, o_s, wo, tile=n_dec)

    wg = w_ff_gate[0].astype(BF16)
    wu = w_ff_up[0].astype(BF16)
    wd = w_ff_down[0].astype(BF16)
    tf = wg.shape[1] // 2
    xp = _ffn(xp, _row(g_ffn[0]), wg, wu, wd, tile=tile_p, tf=tf)
    xs = _ffn(xs, _row(g_ffn[0]), wg, wu, wd, tile=n_dec, tf=tf)

    wpg = w_ple_gate[0].astype(BF16)
    wpi = w_ple_in[0].astype(BF16)
    xp = _ple(xp, _row(g_ple[0]), wpg, p_prompt[0].reshape(T, -1), wpi, tile=tile_p)
    xs = _ple(xs, _row(g_ple[0]), wpg, p_sample[0].reshape(n_dec, -1), wpi, tile=n_dec)

    wqkT = w_qkv_b[0][:, :2 * D].T.astype(BF16)
    wv = w_qkv_b[0][:, 2 * D:].astype(BF16)
    wo = w_o_b[0].astype(BF16)
    qkv_args = (_row(g_mix[1]), wqkT, wv, _col(g_q_b[0]), _col(g_k_b[0]))
    qT_p, kT_p, kTb_p, v_p, vb_p, kmean = _qkv(xp, *qkv_args, cos_p, sin_p, tile=MOBA_BLOCK, with_kmean=True)
    qT_s, kT_s, _, v_s, _ = _qkv(xs, *qkv_args, cos_s, sin_s, tile=n_dec, with_kmean=False)
    kb_p = kT_p.reshape(1, 1, N_HEADS_B, HEAD_DIM, T).transpose(0, 1, 4, 2, 3)
    vb_out_p = v_p.reshape(1, 1, T, N_HEADS_B, HEAD_DIM)
    kb_s = kT_s.T.reshape(1, nb, dec_seq, N_HEADS_B, HEAD_DIM)
    vb_out_s = v_s.reshape(1, nb, dec_seq, N_HEADS_B, HEAD_DIM)

    o_p = _moba_prompt(qT_p, kTb_p, vb_p, _moba_kmean_aug(kmean))

    q4 = qT_s.T.reshape(nb, dec_seq, N_HEADS_B, HEAD_DIM)
    qbd2 = jnp.einsum('bthd,hH->bhtHd', q4, jnp.eye(N_HEADS_B, dtype=F32)).reshape(nb, N_HEADS_B * dec_seq, D)
    kT_pages_b = cache_k_b[0].transpose(0, 2, 3, 1).reshape(n_pool, D, PAGE_SIZE)
    idx = _moba_gate(pt, qbd2, kT_pages_b)[:, :, :MOBA_TOPK]
    idx = idx.reshape(nb, N_HEADS_B, dec_seq, MOBA_TOPK)
    ppb = MOBA_BLOCK // PAGE_SIZE
    logical = (idx[..., None] * ppb + jnp.arange(ppb)).reshape(nb, N_HEADS_B, -1)
    phys = pt[jnp.arange(nb)[:, None, None], logical]
    per_head = lambda aT: aT.reshape(N_HEADS_B, HEAD_DIM, nb, dec_seq).transpose(2, 0, 1, 3)
    qT_h = _pad_lanes(per_head(qT_s) * (HEAD_DIM ** -0.5), PAGE_SIZE)
    knewT = _pad_lanes(per_head(kT_s), PAGE_SIZE)
    vnewT = _pad_lanes(per_head(v_s.T), PAGE_SIZE)
    kT_slices_b = cache_k_b[0].transpose(0, 2, 3, 1)
    vT_slices_b = cache_v_b[0].transpose(0, 2, 3, 1)
    oT = _moba_decode_attend(phys, qT_h, knewT, vnewT, kT_slices_b, vT_slices_b, dec_seq=dec_seq)
    o_s = oT[..., :dec_seq].transpose(0, 3, 1, 2).reshape(n_dec, D).astype(BF16)

    xp = _oproj(xp, o_p, wo, tile=tile_p)
    xs = _oproj(xs, o_s, wo, tile=n_dec)

    wr = _pad_lanes(w_router[0], LANES)
    weg = w_ex_gate[0].astype(BF16)
    weu = w_ex_up[0].astype(BF16)
    wed = w_ex_down[0].astype(BF16)
    tfe = weg.shape[2] // 4
    hn_p, comb_p = _router(xp, _row(g_ffn[1]), wr, tile=tile_p)
    hn_s, comb_s = _router(xs, _row(g_ffn[1]), wr, tile=n_dec)
    xp = _experts(xp, hn_p, comb_p, weg, weu, wed, tile=tile_p, tf=tfe)
    xs = _experts(xs, hn_s, comb_s, weg, weu, wed, tile=n_dec, tf=tfe)

    wpg = w_ple_gate[1].astype(BF16)
    wpi = w_ple_in[1].astype(BF16)
    xp = _ple(xp, _row(g_ple[1]), wpg, p_prompt[1].reshape(T, -1), wpi, tile=tile_p)
    xs = _ple(xs, _row(g_ple[1]), wpg, p_sample[1].reshape(n_dec, -1), wpi, tile=n_dec)

    return (xp.reshape(1, T, D), xs.reshape(nb, dec_seq, D), ka_p, va_p, kb_p, vb_out_p,
            ka_s, va_s, kb_s, vb_out_s)
```

```python
import functools
import math

import jax
import jax.numpy as jnp
from jax import lax
from jax.experimental import pallas as pl
from jax.experimental.pallas import tpu as pltpu

F32 = jnp.float32
BF16 = jnp.bfloat16
HIGHEST = lax.Precision.HIGHEST

D_MODEL = 1024
HEAD_DIM = 64
N_HEADS_A = 8
N_HEADS_B = 16
PAGE_SIZE = 128
MOBA_BLOCK = 256
MOBA_TOPK = 3
ROPE_THETA = 10000.0
N_EXPERTS = 8
EPS = 1e-6
LANES = 128
MASK_NEG = -1e30
KNOCKED = -3e38
SEL_BIAS = float(2 ** 30)
VMEM_LIMIT = 56 * 1024 * 1024

NT_DIMS = (((1,), (1,)), ((), ()))


def _cparams(*sem):
    return pltpu.CompilerParams(dimension_semantics=sem, vmem_limit_bytes=VMEM_LIMIT)


def _rms_rows(x, g):
    ms = jnp.mean(x * x, axis=-1, keepdims=True)
    return x * lax.rsqrt(ms + EPS) * g


def _sigmoid(x):
    return 1.0 / (1.0 + jnp.exp(-x))


def _qkv_body(x_ref, g_ref, wqk_ref, wv_ref, gq_ref, gk_ref, cos_ref, sin_ref,
              qT_ref, kT_ref, kTb_ref, v_ref, vb_ref, *rest, with_kmean):
    if with_kmean:
        kmean_ref, y_scr = rest
    else:
        (y_scr,) = rest
    hn = _rms_rows(x_ref[...], g_ref[...]).astype(BF16)
    v = jnp.dot(hn, wv_ref[...], preferred_element_type=F32)
    v_ref[...] = v
    vb_ref[...] = v.astype(BF16)
    y_scr[...] = lax.dot_general(wqk_ref[...], hn, NT_DIMS, preferred_element_type=F32)
    cos = cos_ref[...]
    sin = sin_ref[...]
    half = HEAD_DIM // 2
    n_groups = D_MODEL // HEAD_DIM
    for grp in range(2 * n_groups):
        y = y_scr[grp * HEAD_DIM:(grp + 1) * HEAD_DIM, :]
        gain = gq_ref[...] if grp < n_groups else gk_ref[...]
        ms = jnp.mean(y * y, axis=0, keepdims=True)
        y = y * lax.rsqrt(ms + EPS) * gain
        rot = jnp.concatenate([-y[half:, :], y[:half, :]], axis=0)
        y = y * cos + rot * sin
        if grp < n_groups:
            qT_ref[grp * HEAD_DIM:(grp + 1) * HEAD_DIM, :] = y
        else:
            r = (grp - n_groups) * HEAD_DIM
            kT_ref[r:r + HEAD_DIM, :] = y
            kTb_ref[r:r + HEAD_DIM, :] = y.astype(BF16)
    if with_kmean:
        t = x_ref.shape[0]
        avg = jnp.full((8, t), 1.0 / t, F32)
        km = lax.dot_general(avg, kT_ref[...], NT_DIMS, precision=HIGHEST,
                             preferred_element_type=F32)
        kmean_ref[0] = km[0:1, :]


def _qkv(x, g, wqkT, wv, gq, gk, cosT, sinT, *, tile, with_kmean):
    T = x.shape[0]
    assert T % tile == 0
    n = T // tile
    out_shape = [
        jax.ShapeDtypeStruct((D_MODEL, T), F32),
        jax.ShapeDtypeStruct((D_MODEL, T), F32),
        jax.ShapeDtypeStruct((D_MODEL, T), BF16),
        jax.ShapeDtypeStruct((T, D_MODEL), F32),
        jax.ShapeDtypeStruct((T, D_MODEL), BF16),
    ]
    colblk = pl.BlockSpec((D_MODEL, tile), lambda i: (0, i))
    rowblk = pl.BlockSpec((tile, D_MODEL), lambda i: (i, 0))
    out_specs = [colblk, colblk, colblk, rowblk, rowblk]
    if with_kmean:
        assert tile == MOBA_BLOCK
        out_shape.append(jax.ShapeDtypeStruct((n, 1, D_MODEL), F32))
        out_specs.append(pl.BlockSpec((1, 1, D_MODEL), lambda i: (i, 0, 0)))
    const2 = lambda i: (0, 0)
    return pl.pallas_call(
        functools.partial(_qkv_body, with_kmean=with_kmean),
        grid=(n,),
        in_specs=[
            rowblk,
            pl.BlockSpec((1, D_MODEL), const2),
            pl.BlockSpec((2 * D_MODEL, D_MODEL), const2),
            pl.BlockSpec((D_MODEL, D_MODEL), const2),
            pl.BlockSpec((HEAD_DIM, 1), const2),
            pl.BlockSpec((HEAD_DIM, 1), const2),
            pl.BlockSpec((HEAD_DIM, tile), lambda i: (0, i)),
            pl.BlockSpec((HEAD_DIM, tile), lambda i: (0, i)),
        ],
        out_specs=out_specs,
        out_shape=out_shape,
        scratch_shapes=[pltpu.VMEM((2 * D_MODEL, tile), F32)],
        compiler_params=_cparams("arbitrary"),
        name="qkv_norm_rope",
    )(x, g, wqkT, wv, gq, gk, cosT, sinT)


def _online_update(s, vv, m_ref, l_ref, acc_ref):
    m_old = m_ref[...]
    m_new = jnp.maximum(m_old, jnp.max(s, axis=1, keepdims=True))
    p = jnp.exp(s - m_new)
    alpha = jnp.exp(m_old - m_new)
    l_ref[...] = alpha * l_ref[...] + jnp.sum(p, axis=1, keepdims=True)
    acc_ref[...] = alpha * acc_ref[...] + jnp.dot(p.astype(BF16), vv, preferred_element_type=F32)
    m_ref[...] = m_new


def _init_softmax_state(m_ref, l_ref, acc_ref):
    m_ref[...] = jnp.full(m_ref.shape, MASK_NEG, F32)
    l_ref[...] = jnp.zeros(l_ref.shape, F32)
    acc_ref[...] = jnp.zeros(acc_ref.shape, F32)


LOG2E = math.log2(math.e)


def _flash_step(s, v_aug, m_ref, acc_ref):
    tk = s.shape[1]
    w = acc_ref.shape[1]
    m_old = m_ref[...]
    m_new = jnp.maximum(m_old, jnp.max(s, axis=1, keepdims=True))
    alpha = jnp.exp2(m_old - m_new)
    p = jnp.exp2(s - jnp.concatenate([m_new] * (tk // LANES), axis=1))
    pv = jnp.dot(p.astype(BF16), v_aug, preferred_element_type=F32)
    if w > LANES:
        alpha = jnp.concatenate([alpha] * (w // LANES), axis=1)
    acc_ref[...] = alpha * acc_ref[...] + pv
    m_ref[...] = m_new


def _init_flash_state(m_ref, acc_ref):
    m_ref[...] = jnp.full(m_ref.shape, MASK_NEG, F32)
    acc_ref[...] = jnp.zeros(acc_ref.shape, F32)


def _diff_lambda(lam_ref, lam_init):
    lf = lam_ref[...]
    a = jnp.sum(lf[0:1, :] * lf[1:2, :], axis=1, keepdims=True)
    b = jnp.sum(lf[2:3, :] * lf[3:4, :], axis=1, keepdims=True)
    return jnp.exp(a) - jnp.exp(b) + lam_init


def _diff_attn_body(qT_ref, kT_ref, v_ref, lam_ref, gsub_ref, o_ref, m1, a1, m2, a2, *, tile,
                    lam_init):
    qi = pl.program_id(1)
    hw = 2 * HEAD_DIM
    q = qT_ref[...].T * (HEAD_DIM ** -0.5 * LOG2E)
    lane = lax.broadcasted_iota(jnp.int32, q.shape, 1)
    q1 = jnp.where(lane < HEAD_DIM, q, 0.0).astype(BF16)
    q2 = jnp.where(lane >= HEAD_DIM, q, 0.0).astype(BF16)
    _init_flash_state(m1, a1)
    _init_flash_state(m2, a2)
    ones = jnp.ones((tile, hw), BF16)

    def step(kb, masked):
        off = pl.multiple_of(kb * tile, tile)
        kt = kT_ref[:, pl.ds(off, tile)]
        v_aug = jnp.concatenate([v_ref[pl.ds(off, tile), :], ones], axis=1)
        s1 = jnp.dot(q1, kt, preferred_element_type=F32)
        s2 = jnp.dot(q2, kt, preferred_element_type=F32)
        if masked:
            row = lax.broadcasted_iota(jnp.int32, (tile, tile), 0)
            col = lax.broadcasted_iota(jnp.int32, (tile, tile), 1)
            keep = col <= row
            s1 = jnp.where(keep, s1, MASK_NEG)
            s2 = jnp.where(keep, s2, MASK_NEG)
        _flash_step(s1, v_aug, m1, a1)
        _flash_step(s2, v_aug, m2, a2)

    def loop_body(kb, c):
        step(kb, False)
        return c

    lax.fori_loop(0, qi, loop_body, 0)
    step(qi, True)

    lam = _diff_lambda(lam_ref, lam_init)
    acc1 = a1[...]
    acc2 = a2[...]
    o = acc1[:, :hw] / acc1[:, hw:] - lam * (acc2[:, :hw] / acc2[:, hw:])
    o = _rms_rows(o, gsub_ref[...]) * (1.0 - lam_init)
    o_ref[...] = o.astype(o_ref.dtype)


def _diff_attn_prompt(qT, kTb, vb, lam, gsub, *, lam_init, tile):
    T = qT.shape[1]
    hw = 2 * HEAD_DIM
    assert T % tile == 0
    return pl.pallas_call(
        functools.partial(_diff_attn_body, tile=tile, lam_init=lam_init),
        grid=(N_HEADS_A, T // tile),
        in_specs=[
            pl.BlockSpec((hw, tile), lambda h, i: (h, i)),
            pl.BlockSpec((hw, T), lambda h, i: (h, 0)),
            pl.BlockSpec((T, hw), lambda h, i: (0, h)),
            pl.BlockSpec((4, HEAD_DIM), lambda h, i: (0, 0)),
            pl.BlockSpec((1, hw), lambda h, i: (0, 0)),
        ],
        out_specs=pl.BlockSpec((tile, hw), lambda h, i: (i, h)),
        out_shape=jax.ShapeDtypeStruct((T, D_MODEL), BF16),
        scratch_shapes=[
            pltpu.VMEM((tile, LANES), F32), pltpu.VMEM((tile, 2 * hw), F32),
            pltpu.VMEM((tile, LANES), F32), pltpu.VMEM((tile, 2 * hw), F32),
        ],
        compiler_params=_cparams("arbitrary", "arbitrary"),
        name="diff_attn_prompt",
    )(qT, kTb, vb, lam, gsub)


def _topk_pick(g, lanef, k):
    picked = jnp.zeros(g.shape, jnp.bool_)
    idxs = []
    for _ in range(k):
        m = jnp.max(g, axis=1, keepdims=True)
        idx = jnp.min(jnp.where(g == m, lanef, float(4 * LANES)), axis=1, keepdims=True)
        hit = lanef == idx
        picked = jnp.logical_or(picked, hit)
        g = jnp.where(hit, KNOCKED, g)
        idxs.append(idx)
    return picked, idxs


def _moba_body(qT_ref, kT_ref, v_ref, km_ref, o_ref, ma, aa, mb, ab, *, tile):
    qi = pl.program_id(1)
    blk_shift = MOBA_BLOCK.bit_length() - 1
    bpt = tile // MOBA_BLOCK
    qf = qT_ref[...].T
    lane = lax.broadcasted_iota(jnp.int32, qf.shape, 1)
    lanef = lane.astype(F32)
    lo = lane < HEAD_DIM
    qs = qf * (HEAD_DIM ** -0.5 * LOG2E)
    own = qi * bpt + jnp.right_shift(lax.broadcasted_iota(jnp.int32, qf.shape, 0), blk_shift)

    def select_bias(hh, base):
        g = jnp.dot(qf, km_ref[hh], precision=HIGHEST, preferred_element_type=F32)
        blk = lane - base
        past = jnp.logical_and(blk >= 0, blk < own)
        g = jnp.where(past, g, MASK_NEG)
        picked, _ = _topk_pick(g, lanef, MOBA_TOPK)
        sel = jnp.logical_or(jnp.logical_and(picked, past), blk == own)
        return jnp.where(sel, 0.0, -SEL_BIAS)

    qa = jnp.where(lo, qs, select_bias(0, HEAD_DIM)).astype(BF16)
    qb = jnp.where(lo, select_bias(1, 0), qs).astype(BF16)
    _init_flash_state(ma, aa)
    _init_flash_state(mb, ab)
    sub = lax.broadcasted_iota(jnp.int32, (HEAD_DIM, tile), 0)
    key_blk = jnp.right_shift(lax.broadcasted_iota(jnp.int32, (HEAD_DIM, tile), 1), blk_shift)
    vlane = lax.broadcasted_iota(jnp.int32, (tile, 2 * HEAD_DIM), 1)
    one = jnp.ones((tile, 2 * HEAD_DIM), BF16)

    def step(kb, masked):
        off = pl.multiple_of(kb * tile, tile)
        kt = kT_ref[:, pl.ds(off, tile)]
        vv = v_ref[pl.ds(off, tile), :]
        onehot = jnp.where(sub == kb * bpt + key_blk, 1.0, 0.0).astype(BF16)
        kta = jnp.concatenate([kt[:HEAD_DIM, :], onehot], axis=0)
        ktb = jnp.concatenate([onehot, kt[HEAD_DIM:, :]], axis=0)
        sa = jnp.dot(qa, kta, preferred_element_type=F32)
        sb = jnp.dot(qb, ktb, preferred_element_type=F32)
        if masked:
            row = lax.broadcasted_iota(jnp.int32, (tile, tile), 0)
            col = lax.broadcasted_iota(jnp.int32, (tile, tile), 1)
            keep = col <= row
            sa = jnp.where(keep, sa, MASK_NEG)
            sb = jnp.where(keep, sb, MASK_NEG)
        _flash_step(sa, jnp.where(vlane < HEAD_DIM, vv, one), ma, aa)
        _flash_step(sb, jnp.where(vlane >= HEAD_DIM, vv, one), mb, ab)

    def loop_body(kb, c):
        step(kb, False)
        return c

    lax.fori_loop(0, qi, loop_body, 0)
    step(qi, True)
    acc_a = aa[...]
    acc_b = ab[...]
    oa = acc_a / pltpu.roll(acc_a, HEAD_DIM, 1)
    ob = acc_b / pltpu.roll(acc_b, HEAD_DIM, 1)
    o_ref[...] = jnp.where(lo, oa, ob).astype(o_ref.dtype)


def _moba_prompt(qT, kTb, vb, km_aug, *, tile):
    T = qT.shape[1]
    hw = 2 * HEAD_DIM
    assert T % tile == 0 and tile % MOBA_BLOCK == 0 and T // MOBA_BLOCK <= HEAD_DIM
    return pl.pallas_call(
        functools.partial(_moba_body, tile=tile),
        grid=(N_HEADS_B // 2, T // tile),
        in_specs=[
            pl.BlockSpec((hw, tile), lambda h, i: (h, i)),
            pl.BlockSpec((hw, T), lambda h, i: (h, 0)),
            pl.BlockSpec((T, hw), lambda h, i: (0, h)),
            pl.BlockSpec((2, hw, hw), lambda h, i: (h, 0, 0)),
        ],
        out_specs=pl.BlockSpec((tile, hw), lambda h, i: (i, h)),
        out_shape=jax.ShapeDtypeStruct((T, D_MODEL), BF16),
        scratch_shapes=[
            pltpu.VMEM((tile, LANES), F32), pltpu.VMEM((tile, hw), F32),
            pltpu.VMEM((tile, LANES), F32), pltpu.VMEM((tile, hw), F32),
        ],
        compiler_params=_cparams("arbitrary", "arbitrary"),
        name="moba_attn_prompt",
    )(qT, kTb, vb, km_aug)


def _moba_kmean_aug(kmean):
    n_blk = kmean.shape[0]
    km = kmean.reshape(n_blk, N_HEADS_B, HEAD_DIM).transpose(1, 2, 0)
    km = jnp.pad(km, ((0, 0), (0, 0), (0, HEAD_DIM - n_blk)))
    z = jnp.zeros_like(km)
    even = jnp.concatenate([jnp.concatenate([z, km], axis=2), jnp.concatenate([z, z], axis=2)], axis=1)
    odd = jnp.concatenate([jnp.concatenate([z, z], axis=2), jnp.concatenate([km, z], axis=2)], axis=1)
    is_even = (jnp.arange(N_HEADS_B) % 2 == 0)[:, None, None]
    return jnp.where(is_even, even, odd)


def _oproj_body(x_ref, o_ref, w_ref, y_ref):
    y_ref[...] = x_ref[...] + jnp.dot(o_ref[...], w_ref[...], preferred_element_type=F32)


def _oproj(x, o, w, *, tile):
    T = x.shape[0]
    rowblk = pl.BlockSpec((tile, D_MODEL), lambda i: (i, 0))
    return pl.pallas_call(
        _oproj_body,
        grid=(T // tile,),
        in_specs=[rowblk, rowblk, pl.BlockSpec((D_MODEL, D_MODEL), lambda i: (0, 0))],
        out_specs=rowblk,
        out_shape=jax.ShapeDtypeStruct((T, D_MODEL), F32),
        compiler_params=_cparams("arbitrary"),
        name="attn_out_proj",
    )(x, o, w)


def _swiglu_chunk(hn, wg, wu, wd):
    a = jnp.dot(hn, wg, preferred_element_type=F32)
    u = jnp.dot(hn, wu, preferred_element_type=F32)
    act = (a * _sigmoid(a) * u).astype(BF16)
    return jnp.dot(act, wd, preferred_element_type=F32)


def _ffn_body(x_ref, g_ref, wg_ref, wu_ref, wd_ref, y_ref, hn_scr, acc_scr):
    f = pl.program_id(1)

    @pl.when(f == 0)
    def _():
        x = x_ref[...]
        hn_scr[...] = _rms_rows(x, g_ref[...]).astype(BF16)
        acc_scr[...] = x

    acc_scr[...] += _swiglu_chunk(hn_scr[...], wg_ref[...], wu_ref[...], wd_ref[...])

    @pl.when(f == pl.num_programs(1) - 1)
    def _():
        y_ref[...] = acc_scr[...]


def _ffn(x, g, wg, wu, wd, *, tile, tf):
    T = x.shape[0]
    F = wg.shape[1]
    assert F % tf == 0
    rowblk = pl.BlockSpec((tile, D_MODEL), lambda i, f: (i, 0))
    return pl.pallas_call(
        _ffn_body,
        grid=(T // tile, F // tf),
        in_specs=[
            rowblk,
            pl.BlockSpec((1, D_MODEL), lambda i, f: (0, 0)),
            pl.BlockSpec((D_MODEL, tf), lambda i, f: (0, f)),
            pl.BlockSpec((D_MODEL, tf), lambda i, f: (0, f)),
            pl.BlockSpec((tf, D_MODEL), lambda i, f: (f, 0)),
        ],
        out_specs=rowblk,
        out_shape=jax.ShapeDtypeStruct((T, D_MODEL), F32),
        scratch_shapes=[pltpu.VMEM((tile, D_MODEL), BF16), pltpu.VMEM((tile, D_MODEL), F32)],
        compiler_params=_cparams("arbitrary", "arbitrary"),
        name="dense_swiglu",
    )(x, g, wg, wu, wd)


def _router_body(x_ref, g_ref, wr_ref, hn_ref, comb_ref):
    hf = _rms_rows(x_ref[...], g_ref[...])
    hn_ref[...] = hf.astype(BF16)
    logits = jnp.dot(hf, wr_ref[...], precision=HIGHEST, preferred_element_type=F32)
    lane = lax.broadcasted_iota(jnp.int32, logits.shape, 1)
    lanef = lane.astype(F32)
    logits = jnp.where(lane < N_EXPERTS, logits, MASK_NEG)
    v1 = jnp.max(logits, axis=1, keepdims=True)
    i1 = jnp.min(jnp.where(logits == v1, lanef, float(LANES)), axis=1, keepdims=True)
    rest = jnp.where(lanef == i1, KNOCKED, logits)
    v2 = jnp.max(rest, axis=1, keepdims=True)
    i2 = jnp.min(jnp.where(rest == v2, lanef, float(LANES)), axis=1, keepdims=True)
    e2 = jnp.exp(v2 - v1)
    g1 = 1.0 / (1.0 + e2)
    g2 = e2 / (1.0 + e2)
    comb_ref[...] = jnp.where(lanef == i1, g1, 0.0) + jnp.where(lanef == i2, g2, 0.0)


def _router(x, g, wr, *, tile):
    T = x.shape[0]
    rowblk = pl.BlockSpec((tile, D_MODEL), lambda i: (i, 0))
    return pl.pallas_call(
        _router_body,
        grid=(T // tile,),
        in_specs=[rowblk, pl.BlockSpec((1, D_MODEL), lambda i: (0, 0)),
                  pl.BlockSpec((D_MODEL, LANES), lambda i: (0, 0))],
        out_specs=[rowblk, pl.BlockSpec((tile, LANES), lambda i: (i, 0))],
        out_shape=[jax.ShapeDtypeStruct((T, D_MODEL), BF16), jax.ShapeDtypeStruct((T, LANES), F32)],
        compiler_params=_cparams("arbitrary"),
        name="moe_router",
    )(x, g, wr)


def _experts_body(x_ref, hn_ref, comb_ref, wg_ref, wu_ref, wd_ref, y_ref, acc_scr):
    e = pl.program_id(1)
    f = pl.program_id(2)

    @pl.when(jnp.logical_and(e == 0, f == 0))
    def _():
        acc_scr[...] = x_ref[...]

    comb = comb_ref[...]
    lane = lax.broadcasted_iota(jnp.int32, comb.shape, 1)
    w_e = jnp.sum(jnp.where(lane == e, comb, 0.0), axis=1, keepdims=True)
    acc_scr[...] += w_e * _swiglu_chunk(hn_ref[...], wg_ref[...], wu_ref[...], wd_ref[...])

    @pl.when(jnp.logical_and(e == pl.num_programs(1) - 1, f == pl.num_programs(2) - 1))
    def _():
        y_ref[...] = acc_scr[...]


def _experts(x, hn, comb, wg, wu, wd, *, tile, tf):
    T = x.shape[0]
    E, _, F = wg.shape
    assert F % tf == 0
    rowblk = pl.BlockSpec((tile, D_MODEL), lambda i, e, f: (i, 0))
    return pl.pallas_call(
        _experts_body,
        grid=(T // tile, E, F // tf),
        in_specs=[
            rowblk, rowblk,
            pl.BlockSpec((tile, LANES), lambda i, e, f: (i, 0)),
            pl.BlockSpec((None, D_MODEL, tf), lambda i, e, f: (e, 0, f)),
            pl.BlockSpec((None, D_MODEL, tf), lambda i, e, f: (e, 0, f)),
            pl.BlockSpec((None, tf, D_MODEL), lambda i, e, f: (e, f, 0)),
        ],
        out_specs=rowblk,
        out_shape=jax.ShapeDtypeStruct((T, D_MODEL), F32),
        scratch_shapes=[pltpu.VMEM((tile, D_MODEL), F32)],
        compiler_params=_cparams("arbitrary", "arbitrary", "arbitrary"),
        name="moe_experts",
    )(x, hn, comb, wg, wu, wd)


def _ple_body(x_ref, g_ref, wgate_ref, p_ref, win_ref, y_ref):
    x = x_ref[...]
    hn = _rms_rows(x, g_ref[...]).astype(BF16)
    gate = _sigmoid(jnp.dot(hn, wgate_ref[...], preferred_element_type=F32))
    emb = jnp.dot(p_ref[...].astype(BF16), win_ref[...], preferred_element_type=F32)
    y_ref[...] = x + gate * emb


def _ple(x, g, wgate, p, win, *, tile):
    T = x.shape[0]
    pd = p.shape[1]
    rowblk = pl.BlockSpec((tile, D_MODEL), lambda i: (i, 0))
    return pl.pallas_call(
        _ple_body,
        grid=(T // tile,),
        in_specs=[
            rowblk,
            pl.BlockSpec((1, D_MODEL), lambda i: (0, 0)),
            pl.BlockSpec((D_MODEL, D_MODEL), lambda i: (0, 0)),
            pl.BlockSpec((tile, pd), lambda i: (i, 0)),
            pl.BlockSpec((pd, D_MODEL), lambda i: (0, 0)),
        ],
        out_specs=rowblk,
        out_shape=jax.ShapeDtypeStruct((T, D_MODEL), F32),
        compiler_params=_cparams("arbitrary"),
        name="per_layer_embedding",
    )(x, g, wgate, p, win)


PAGES_PER_STEP = 8


def _diff_decode_body(pt_ref, qbd_ref, knew_ref, vnew_ref, lam_ref, gsub_ref, *rest,
                      dec_seq, lam_init):
    kT_refs = rest[:PAGES_PER_STEP]
    v_refs = rest[PAGES_PER_STEP:2 * PAGES_PER_STEP]
    o_ref, m_ref, l_ref, acc_ref = rest[2 * PAGES_PER_STEP:]
    j = pl.program_id(1)
    n_rows = qbd_ref.shape[0]

    @pl.when(j == 0)
    def _():
        _init_softmax_state(m_ref, l_ref, acc_ref)

    qbd = qbd_ref[...]
    s = jnp.concatenate(
        [jnp.dot(qbd, r[...].astype(BF16), preferred_element_type=F32) for r in kT_refs], axis=1)
    m_old = m_ref[...]
    m_new = jnp.maximum(m_old, jnp.max(s, axis=1, keepdims=True))
    p = jnp.exp(s - m_new)
    alpha = jnp.exp(m_old - m_new)
    l_ref[...] = alpha * l_ref[...] + jnp.sum(p, axis=1, keepdims=True)
    acc = alpha * acc_ref[...]
    for i, r in enumerate(v_refs):
        vcat = jnp.concatenate(
            [r[pl.ds(h, PAGE_SIZE, stride=N_HEADS_A), :] for h in range(N_HEADS_A)], axis=1)
        acc += jnp.dot(p[:, i * PAGE_SIZE:(i + 1) * PAGE_SIZE].astype(BF16), vcat.astype(BF16),
                       preferred_element_type=F32)
    acc_ref[...] = acc
    m_ref[...] = m_new

    @pl.when(j == pl.num_programs(1) - 1)
    def _():
        sn = lax.dot_general(qbd, knew_ref[...], NT_DIMS, preferred_element_type=F32)
        row = lax.broadcasted_iota(jnp.int32, sn.shape, 0)
        col = lax.broadcasted_iota(jnp.int32, sn.shape, 1)
        half = n_rows // 2
        t_of_row = jnp.right_shift(jnp.bitwise_and(row, half - 1), 3)
        sn = jnp.where(jnp.logical_and(col <= t_of_row, col < dec_seq), sn, MASK_NEG)
        _online_update(sn, vnew_ref[...], m_ref, l_ref, acc_ref)
        o = acc_ref[...] / l_ref[...]
        lam = _diff_lambda(lam_ref, lam_init)
        od = o[:half, :] - lam * o[half:, :]
        r2 = lax.broadcasted_iota(jnp.int32, od.shape, 0)
        c2 = lax.broadcasted_iota(jnp.int32, od.shape, 1)
        own = jnp.right_shift(c2, 7) == jnp.bitwise_and(r2, N_HEADS_A - 1)
        od = jnp.where(own, od, 0.0)
        ms = jnp.sum(od * od, axis=1, keepdims=True) / (2 * HEAD_DIM)
        od = od * lax.rsqrt(ms + EPS) * gsub_ref[...] * (1.0 - lam_init)
        rows = [jnp.sum(od[t * N_HEADS_A:(t + 1) * N_HEADS_A, :], axis=0, keepdims=True)
                for t in range(dec_seq)]
        o_ref[...] = jnp.concatenate(rows, axis=0)


def _diff_decode(page_table, qbd, knew, vnew, lam, gsub_tiled, kT_pages, v_rows, *, dec_seq,
                 lam_init):
    nb, n_pages = page_table.shape
    assert n_pages % PAGES_PER_STEP == 0
    n_rows = qbd.shape[1]
    kv_rows = PAGE_SIZE * N_HEADS_A

    def k_spec(i):
        return pl.BlockSpec((None, D_MODEL, PAGE_SIZE),
                            lambda b, j, pt: (pt[b, j * PAGES_PER_STEP + i], 0, 0))

    def v_spec(i):
        return pl.BlockSpec((kv_rows, 2 * HEAD_DIM),
                            lambda b, j, pt: (pt[b, j * PAGES_PER_STEP + i], 0))

    per_seq = lambda b, j, pt: (b, 0, 0)
    const2 = lambda b, j, pt: (0, 0)
    grid_spec = pltpu.PrefetchScalarGridSpec(
        num_scalar_prefetch=1,
        grid=(nb, n_pages // PAGES_PER_STEP),
        in_specs=[
            pl.BlockSpec((None, n_rows, D_MODEL), per_seq),
            pl.BlockSpec((None, PAGE_SIZE, D_MODEL), per_seq),
            pl.BlockSpec((None, PAGE_SIZE, D_MODEL), per_seq),
            pl.BlockSpec((4, HEAD_DIM), const2),
            pl.BlockSpec((1, D_MODEL), const2),
        ] + [k_spec(i) for i in range(PAGES_PER_STEP)] + [v_spec(i) for i in range(PAGES_PER_STEP)],
        out_specs=pl.BlockSpec((None, dec_seq, D_MODEL), per_seq),
        scratch_shapes=[pltpu.VMEM((n_rows, 1), F32), pltpu.VMEM((n_rows, 1), F32),
                        pltpu.VMEM((n_rows, D_MODEL), F32)],
    )
    return pl.pallas_call(
        functools.partial(_diff_decode_body, dec_seq=dec_seq, lam_init=lam_init),
        grid_spec=grid_spec,
        out_shape=jax.ShapeDtypeStruct((nb, dec_seq, D_MODEL), F32),
        compiler_params=_cparams("arbitrary", "arbitrary"),
        name="diff_attn_decode",
    )(page_table, qbd, knew, vnew, lam, gsub_tiled,
      *([kT_pages] * PAGES_PER_STEP), *([v_rows] * PAGES_PER_STEP))


def _moba_gate_body(pt_ref, qbd_ref, *rest):
    kT_refs = rest[:PAGES_PER_STEP]
    idx_ref, km_scr = rest[PAGES_PER_STEP:]
    j = pl.program_id(1)
    pages_per_block = MOBA_BLOCK // PAGE_SIZE
    blocks_per_step = PAGES_PER_STEP // pages_per_block

    @pl.when(j == 0)
    def _():
        km_scr[...] = jnp.zeros(km_scr.shape, F32)

    lane = lax.broadcasted_iota(jnp.int32, km_scr.shape, 1)
    km = km_scr[...]
    for bb in range(blocks_per_step):
        tot = jnp.sum(kT_refs[bb * pages_per_block][...], axis=1, keepdims=True)
        for i in range(1, pages_per_block):
            tot = tot + jnp.sum(kT_refs[bb * pages_per_block + i][...], axis=1, keepdims=True)
        km = jnp.where(lane == j * blocks_per_step + bb, tot * (1.0 / MOBA_BLOCK), km)
    km_scr[...] = km

    @pl.when(j == pl.num_programs(1) - 1)
    def _():
        n_blk = pl.num_programs(1) * blocks_per_step
        g = jnp.dot(qbd_ref[...], km_scr[...], precision=HIGHEST, preferred_element_type=F32)
        glane = lax.broadcasted_iota(jnp.int32, g.shape, 1)
        glanef = glane.astype(F32)
        g = jnp.where(glane < n_blk, g, MASK_NEG)
        _, idxs = _topk_pick(g, glanef, MOBA_TOPK)
        out = jnp.zeros(g.shape, F32)
        for r, idx in enumerate(idxs):
            out = jnp.where(glane == r, idx, out)
        idx_ref[...] = out.astype(jnp.int32)


def _moba_gate(page_table, qbd, kT_pages):
    nb, n_pages = page_table.shape
    assert n_pages % PAGES_PER_STEP == 0 and n_pages * PAGE_SIZE // MOBA_BLOCK <= LANES
    n_rows = qbd.shape[1]

    def k_spec(i):
        return pl.BlockSpec((None, D_MODEL, PAGE_SIZE),
                            lambda b, j, pt: (pt[b, j * PAGES_PER_STEP + i], 0, 0))

    per_seq = lambda b, j, pt: (b, 0, 0)
    grid_spec = pltpu.PrefetchScalarGridSpec(
        num_scalar_prefetch=1,
        grid=(nb, n_pages // PAGES_PER_STEP),
        in_specs=[pl.BlockSpec((None, n_rows, D_MODEL), per_seq)]
        + [k_spec(i) for i in range(PAGES_PER_STEP)],
        out_specs=pl.BlockSpec((None, n_rows, LANES), per_seq),
        scratch_shapes=[pltpu.VMEM((D_MODEL, LANES), F32)],
    )
    return pl.pallas_call(
        _moba_gate_body,
        grid_spec=grid_spec,
        out_shape=jax.ShapeDtypeStruct((nb, n_rows, LANES), jnp.int32),
        compiler_params=_cparams("arbitrary", "arbitrary"),
        name="moba_decode_gate",
    )(page_table, qbd, *([kT_pages] * PAGES_PER_STEP))


def _moba_sel_body(phys_ref, q_ref, knew_ref, vnew_ref, *rest, dec_seq, n_sel):
    k_refs = rest[:dec_seq * n_sel]
    v_refs = rest[dec_seq * n_sel:2 * dec_seq * n_sel]
    (o_ref,) = rest[2 * dec_seq * n_sel:]
    q = q_ref[...]
    lane = lax.broadcasted_iota(jnp.int32, (1, PAGE_SIZE), 1)
    olane = lax.broadcasted_iota(jnp.int32, o_ref.shape, 1)
    out = jnp.zeros(o_ref.shape, F32)
    for t in range(dec_seq):
        qcol = q[:, t:t + 1]
        ks = [k_refs[t * n_sel + s][...] for s in range(n_sel)] + [knew_ref[...]]
        vs = [v_refs[t * n_sel + s][...] for s in range(n_sel)] + [vnew_ref[...]]
        scores = [jnp.sum(k * qcol, axis=0, keepdims=True) for k in ks]
        scores[-1] = jnp.where(lane <= t, scores[-1], MASK_NEG)
        m = scores[0].max(axis=1, keepdims=True)
        for sc in scores[1:]:
            m = jnp.maximum(m, sc.max(axis=1, keepdims=True))
        ps = [jnp.exp(sc - m) for sc in scores]
        l = ps[0].sum(axis=1, keepdims=True)
        for pp in ps[1:]:
            l = l + pp.sum(axis=1, keepdims=True)
        o = jnp.sum(vs[0] * ps[0], axis=1, keepdims=True)
        for vv, pp in zip(vs[1:], ps[1:]):
            o = o + jnp.sum(vv * pp, axis=1, keepdims=True)
        out = jnp.where(olane == t, o / l, out)
    o_ref[...] = out


def _moba_decode_attend(phys, qT, knewT, vnewT, kT_slices, vT_slices, *, dec_seq):
    nb, nh, n_slices = phys.shape
    n_sel = n_slices // dec_seq

    def sel_spec(s):
        return pl.BlockSpec((None, None, HEAD_DIM, PAGE_SIZE),
                            lambda b, h, ph: (ph[b, h, s], h, 0, 0))

    per_head = pl.BlockSpec((None, None, HEAD_DIM, PAGE_SIZE), lambda b, h, ph: (b, h, 0, 0))
    grid_spec = pltpu.PrefetchScalarGridSpec(
        num_scalar_prefetch=1,
        grid=(nb, nh),
        in_specs=[per_head, per_head, per_head]
        + [sel_spec(s) for s in range(n_slices)] + [sel_spec(s) for s in range(n_slices)],
        out_specs=per_head,
    )
    return pl.pallas_call(
        functools.partial(_moba_sel_body, dec_seq=dec_seq, n_sel=n_sel),
        grid_spec=grid_spec,
        out_shape=jax.ShapeDtypeStruct((nb, nh, HEAD_DIM, PAGE_SIZE), F32),
        compiler_params=_cparams("arbitrary", "arbitrary"),
        name="moba_decode_attend",
    )(phys, qT, knewT, vnewT, *([kT_slices] * n_slices), *([vT_slices] * n_slices))


def _rope_tables(pos):
    inv = ROPE_THETA ** (-jnp.arange(0, HEAD_DIM, 2, dtype=F32) / HEAD_DIM)
    ang = pos.astype(F32)[:, None] * inv[None, :]
    ang = jnp.concatenate([ang, ang], axis=-1)
    return jnp.cos(ang).T, jnp.sin(ang).T


def _col(v):
    return v.reshape(-1, 1)


def _row(v):
    return v.reshape(1, -1)


def _pad_lanes(a, n):
    return jnp.pad(a, [(0, 0)] * (a.ndim - 1) + [(0, n - a.shape[-1])])


def kernel(x_prompt, x_sample, cache_k_a, cache_v_a, cache_k_b, cache_v_b, page_table, p_prompt, p_sample,
           g_mix, g_ffn, g_ple, w_ple_gate, w_ple_in, w_qkv_a, g_q_a, g_k_a, lam_a, g_sub_a, w_o_a,
           w_qkv_b, g_q_b, g_k_b, w_o_b, w_ff_gate, w_ff_up, w_ff_down, w_router, w_ex_gate, w_ex_up, w_ex_down):
    _, T, D = x_prompt.shape
    nb, dec_seq, _ = x_sample.shape
    n_dec = nb * dec_seq
    n_pool = cache_k_a.shape[1]
    past_len = page_table.shape[1] * PAGE_SIZE
    assert D == D_MODEL and n_dec % 8 == 0 and dec_seq <= PAGE_SIZE
    pt = page_table.astype(jnp.int32)

    tile_p = 512 if T % 512 == 0 else MOBA_BLOCK
    xp = x_prompt.reshape(T, D)
    xs = x_sample.reshape(n_dec, D)
    cos_p, sin_p = _rope_tables(jnp.arange(T))
    pos_s = past_len + jnp.tile(jnp.arange(dec_seq), nb)
    cos_s, sin_s = _rope_tables(pos_s)

    lam_init = 0.8 - 0.6 * math.exp(-0.3 * 0)
    wqkT = w_qkv_a[0][:, :2 * D].T.astype(BF16)
    wv = w_qkv_a[0][:, 2 * D:].astype(BF16)
    wo = w_o_a[0].astype(BF16)
    qkv_args = (_row(g_mix[0]), wqkT, wv, _col(g_q_a[0]), _col(g_k_a[0]))
    qT_p, kT_p, kTb_p, v_p, vb_p = _qkv(xp, *qkv_args, cos_p, sin_p, tile=MOBA_BLOCK, with_kmean=False)
    qT_s, kT_s, _, v_s, _ = _qkv(xs, *qkv_args, cos_s, sin_s, tile=n_dec, with_kmean=False)
    ka_p = kT_p.reshape(1, 1, N_HEADS_A, 2, HEAD_DIM, T).transpose(0, 1, 5, 2, 3, 4)
    va_p = v_p.reshape(1, 1, T, N_HEADS_A, 2 * HEAD_DIM)
    ka_s = kT_s.T.reshape(1, nb, dec_seq, N_HEADS_A, 2, HEAD_DIM)
    va_s = v_s.reshape(1, nb, dec_seq, N_HEADS_A, 2 * HEAD_DIM)

    o_p = _diff_attn_prompt(qT_p, kTb_p, vb_p, lam_a[0], _row(g_sub_a[0]), lam_init=lam_init,
                            tile=tile_p)

    q5 = (qT_s.T * (HEAD_DIM ** -0.5)).reshape(nb, dec_seq, N_HEADS_A, 2, HEAD_DIM)
    eye_h = jnp.eye(N_HEADS_A, dtype=F32)
    eye_c = jnp.eye(2, dtype=F32)
    qbd = jnp.einsum('bthcd,hH,cC->bcthHCd', q5, eye_h, eye_c).reshape(nb, 2 * dec_seq * N_HEADS_A, D)
    qbd = qbd.astype(BF16)
    knew = jnp.pad(kT_s.T.reshape(nb, dec_seq, D), ((0, 0), (0, PAGE_SIZE - dec_seq), (0, 0))).astype(BF16)
    vnew = jnp.pad(v_s.reshape(nb, dec_seq, D), ((0, 0), (0, PAGE_SIZE - dec_seq), (0, 0))).astype(BF16)
    kT_pages_a = cache_k_a[0].transpose(0, 2, 3, 4, 1).reshape(n_pool, D, PAGE_SIZE)
    v_rows_a = cache_v_a[0].reshape(n_pool * PAGE_SIZE * N_HEADS_A, 2 * HEAD_DIM)
    o_s = _diff_decode(pt, qbd, knew, vnew, lam_a[0], _row(jnp.tile(g_sub_a[0], N_HEADS_A)),
                       kT_pages_a, v_rows_a, dec_seq=dec_seq, lam_init=lam_init)
    o_s = o_s.reshape(n_dec, D).astype(BF16)

    xp = _oproj(xp, o_p, wo, tile=tile_p)
    xs = _oproj(xs, o_s, wo, tile=n_dec)

    wg = w_ff_gate[0].astype(BF16)
    wu = w_ff_up[0].astype(BF16)
    wd = w_ff_down[0].astype(BF16)
    tf = wg.shape[1] // 2
    xp = _ffn(xp, _row(g_ffn[0]), wg, wu, wd, tile=tile_p, tf=tf)
    xs = _ffn(xs, _row(g_ffn[0]), wg, wu, wd, tile=n_dec, tf=tf)

    wpg = w_ple_gate[0].astype(BF16)
    wpi = w_ple_in[0].astype(BF16)
    xp = _ple(xp, _row(g_ple[0]), wpg, p_prompt[0].reshape(T, -1), wpi, tile=tile_p)
    xs = _ple(xs, _row(g_ple[0]), wpg, p_sample[0].reshape(n_dec, -1), wpi, tile=n_dec)

    wqkT = w_qkv_b[0][:, :2 * D].T.astype(BF16)
    wv = w_qkv_b[0][:, 2 * D:].astype(BF16)
    wo = w_o_b[0].astype(BF16)
    qkv_args = (_row(g_mix[1]), wqkT, wv, _col(g_q_b[0]), _col(g_k_b[0]))
    qT_p, kT_p, kTb_p, v_p, vb_p, kmean = _qkv(xp, *qkv_args, cos_p, sin_p, tile=MOBA_BLOCK, with_kmean=True)
    qT_s, kT_s, _, v_s, _ = _qkv(xs, *qkv_args, cos_s, sin_s, tile=n_dec, with_kmean=False)
    kb_p = kT_p.reshape(1, 1, N_HEADS_B, HEAD_DIM, T).transpose(0, 1, 4, 2, 3)
    vb_out_p = v_p.reshape(1, 1, T, N_HEADS_B, HEAD_DIM)
    kb_s = kT_s.T.reshape(1, nb, dec_seq, N_HEADS_B, HEAD_DIM)
    vb_out_s = v_s.reshape(1, nb, dec_seq, N_HEADS_B, HEAD_DIM)

    o_p = _moba_prompt(qT_p, kTb_p, vb_p, _moba_kmean_aug(kmean), tile=tile_p)

    q4 = qT_s.T.reshape(nb, dec_seq, N_HEADS_B, HEAD_DIM)
    qbd2 = jnp.einsum('bthd,hH->bhtHd', q4, jnp.eye(N_HEADS_B, dtype=F32)).reshape(nb, N_HEADS_B * dec_seq, D)
    kT_pages_b = cache_k_b[0].transpose(0, 2, 3, 1).reshape(n_pool, D, PAGE_SIZE)
    idx = _moba_gate(pt, qbd2, kT_pages_b)[:, :, :MOBA_TOPK]
    idx = idx.reshape(nb, N_HEADS_B, dec_seq, MOBA_TOPK)
    ppb = MOBA_BLOCK // PAGE_SIZE
    logical = (idx[..., None] * ppb + jnp.arange(ppb)).reshape(nb, N_HEADS_B, -1)
    phys = pt[jnp.arange(nb)[:, None, None], logical]
    per_head = lambda aT: aT.reshape(N_HEADS_B, HEAD_DIM, nb, dec_seq).transpose(2, 0, 1, 3)
    qT_h = _pad_lanes(per_head(qT_s) * (HEAD_DIM ** -0.5), PAGE_SIZE)
    knewT = _pad_lanes(per_head(kT_s), PAGE_SIZE)
    vnewT = _pad_lanes(per_head(v_s.T), PAGE_SIZE)
    kT_slices_b = cache_k_b[0].transpose(0, 2, 3, 1)
    vT_slices_b = cache_v_b[0].transpose(0, 2, 3, 1)
    oT = _moba_decode_attend(phys, qT_h, knewT, vnewT, kT_slices_b, vT_slices_b, dec_seq=dec_seq)
    o_s = oT[..., :dec_seq].transpose(0, 3, 1, 2).reshape(n_dec, D).astype(BF16)

    xp = _oproj(xp, o_p, wo, tile=tile_p)
    xs = _oproj(xs, o_s, wo, tile=n_dec)

    wr = _pad_lanes(w_router[0], LANES)
    weg = w_ex_gate[0].astype(BF16)
    weu = w_ex_up[0].astype(BF16)
    wed = w_ex_down[0].astype(BF16)
    tfe = weg.shape[2] // 4
    hn_p, comb_p = _router(xp, _row(g_ffn[1]), wr, tile=tile_p)
    hn_s, comb_s = _router(xs, _row(g_ffn[1]), wr, tile=n_dec)
    xp = _experts(xp, hn_p, comb_p, weg, weu, wed, tile=tile_p, tf=tfe)
    xs = _experts(xs, hn_s, comb_s, weg, weu, wed, tile=n_dec, tf=tfe)

    wpg = w_ple_gate[1].astype(BF16)
    wpi = w_ple_in[1].astype(BF16)
    xp = _ple(xp, _row(g_ple[1]), wpg, p_prompt[1].reshape(T, -1), wpi, tile=tile_p)
    xs = _ple(xs, _row(g_ple[1]), wpg, p_sample[1].reshape(n_dec, -1), wpi, tile=n_dec)

    return (xp.reshape(1, T, D), xs.reshape(nb, dec_seq, D), ka_p, va_p, kb_p, vb_out_p,
            ka_s, va_s, kb_s, vb_out_s)
```

```python
import functools
import math

import jax
import jax.numpy as jnp
from jax import lax
from jax.experimental import pallas as pl
from jax.experimental.pallas import tpu as pltpu

F32 = jnp.float32
BF16 = jnp.bfloat16
HIGHEST = lax.Precision.HIGHEST

D_MODEL = 1024
HEAD_DIM = 64
N_HEADS_A = 8
N_HEADS_B = 16
PAGE_SIZE = 128
MOBA_BLOCK = 256
MOBA_TOPK = 3
ROPE_THETA = 10000.0
N_EXPERTS = 8
EPS = 1e-6
LANES = 128
MASK_NEG = -1e30
KNOCKED = -3e38
SEL_BIAS = float(2 ** 30)
VMEM_LIMIT = 56 * 1024 * 1024

NT_DIMS = (((1,), (1,)), ((), ()))


def _cparams(*sem):
    return pltpu.CompilerParams(dimension_semantics=sem, vmem_limit_bytes=VMEM_LIMIT)


def _rms_rows(x, g):
    ms = jnp.mean(x * x, axis=-1, keepdims=True)
    return x * lax.rsqrt(ms + EPS) * g


def _sigmoid(x):
    return 1.0 / (1.0 + jnp.exp(-x))


def _qkv_body(x_ref, g_ref, wqk_ref, wv_ref, gq_ref, gk_ref, cos_ref, sin_ref,
              qT_ref, kT_ref, kTb_ref, v_ref, vb_ref, *rest, with_kmean):
    if with_kmean:
        kmean_ref, y_scr = rest
    else:
        (y_scr,) = rest
    hn = _rms_rows(x_ref[...], g_ref[...]).astype(BF16)
    v = jnp.dot(hn, wv_ref[...], preferred_element_type=F32)
    v_ref[...] = v
    vb_ref[...] = v.astype(BF16)
    y_scr[...] = lax.dot_general(wqk_ref[...], hn, NT_DIMS, preferred_element_type=F32)
    cos = cos_ref[...]
    sin = sin_ref[...]
    half = HEAD_DIM // 2
    n_groups = D_MODEL // HEAD_DIM
    for grp in range(2 * n_groups):
        y = y_scr[grp * HEAD_DIM:(grp + 1) * HEAD_DIM, :]
        gain = gq_ref[...] if grp < n_groups else gk_ref[...]
        ms = jnp.mean(y * y, axis=0, keepdims=True)
        y = y * lax.rsqrt(ms + EPS) * gain
        rot = jnp.concatenate([-y[half:, :], y[:half, :]], axis=0)
        y = y * cos + rot * sin
        if grp < n_groups:
            qT_ref[grp * HEAD_DIM:(grp + 1) * HEAD_DIM, :] = y
        else:
            r = (grp - n_groups) * HEAD_DIM
            kT_ref[r:r + HEAD_DIM, :] = y
            kTb_ref[r:r + HEAD_DIM, :] = y.astype(BF16)
    if with_kmean:
        t = x_ref.shape[0]
        avg = jnp.full((8, t), 1.0 / t, F32)
        km = lax.dot_general(avg, kT_ref[...], NT_DIMS, precision=HIGHEST,
                             preferred_element_type=F32)
        kmean_ref[0] = km[0:1, :]


def _qkv(x, g, wqkT, wv, gq, gk, cosT, sinT, *, tile, with_kmean):
    T = x.shape[0]
    assert T % tile == 0
    n = T // tile
    out_shape = [
        jax.ShapeDtypeStruct((D_MODEL, T), F32),
        jax.ShapeDtypeStruct((D_MODEL, T), F32),
        jax.ShapeDtypeStruct((D_MODEL, T), BF16),
        jax.ShapeDtypeStruct((T, D_MODEL), F32),
        jax.ShapeDtypeStruct((T, D_MODEL), BF16),
    ]
    colblk = pl.BlockSpec((D_MODEL, tile), lambda i: (0, i))
    rowblk = pl.BlockSpec((tile, D_MODEL), lambda i: (i, 0))
    out_specs = [colblk, colblk, colblk, rowblk, rowblk]
    if with_kmean:
        assert tile == MOBA_BLOCK
        out_shape.append(jax.ShapeDtypeStruct((n, 1, D_MODEL), F32))
        out_specs.append(pl.BlockSpec((1, 1, D_MODEL), lambda i: (i, 0, 0)))
    const2 = lambda i: (0, 0)
    return pl.pallas_call(
        functools.partial(_qkv_body, with_kmean=with_kmean),
        grid=(n,),
        in_specs=[
            rowblk,
            pl.BlockSpec((1, D_MODEL), const2),
            pl.BlockSpec((2 * D_MODEL, D_MODEL), const2),
            pl.BlockSpec((D_MODEL, D_MODEL), const2),
            pl.BlockSpec((HEAD_DIM, 1), const2),
            pl.BlockSpec((HEAD_DIM, 1), const2),
            pl.BlockSpec((HEAD_DIM, tile), lambda i: (0, i)),
            pl.BlockSpec((HEAD_DIM, tile), lambda i: (0, i)),
        ],
        out_specs=out_specs,
        out_shape=out_shape,
        scratch_shapes=[pltpu.VMEM((2 * D_MODEL, tile), F32)],
        compiler_params=_cparams("arbitrary"),
        name="qkv_norm_rope",
    )(x, g, wqkT, wv, gq, gk, cosT, sinT)


def _online_update(s, vv, m_ref, l_ref, acc_ref):
    m_old = m_ref[...]
    m_new = jnp.maximum(m_old, jnp.max(s, axis=1, keepdims=True))
    p = jnp.exp(s - m_new)
    alpha = jnp.exp(m_old - m_new)
    l_ref[...] = alpha * l_ref[...] + jnp.sum(p, axis=1, keepdims=True)
    acc_ref[...] = alpha * acc_ref[...] + jnp.dot(p.astype(BF16), vv, preferred_element_type=F32)
    m_ref[...] = m_new


def _init_softmax_state(m_ref, l_ref, acc_ref):
    m_ref[...] = jnp.full(m_ref.shape, MASK_NEG, F32)
    l_ref[...] = jnp.zeros(l_ref.shape, F32)
    acc_ref[...] = jnp.zeros(acc_ref.shape, F32)


LOG2E = math.log2(math.e)


def _flash_step(s, v_aug, m_ref, acc_ref):
    tk = s.shape[1]
    w = acc_ref.shape[1]
    m_old = m_ref[...]
    m_new = jnp.maximum(m_old, jnp.max(s, axis=1, keepdims=True))
    alpha = jnp.exp2(m_old - m_new)
    p = jnp.exp2(s - jnp.concatenate([m_new] * (tk // LANES), axis=1))
    pv = jnp.dot(p.astype(BF16), v_aug, preferred_element_type=F32)
    if w > LANES:
        alpha = jnp.concatenate([alpha] * (w // LANES), axis=1)
    acc_ref[...] = alpha * acc_ref[...] + pv
    m_ref[...] = m_new


def _init_flash_state(m_ref, acc_ref):
    m_ref[...] = jnp.full(m_ref.shape, MASK_NEG, F32)
    acc_ref[...] = jnp.zeros(acc_ref.shape, F32)


def _run_pipelined(n_full, scores, consume, s_a, s_b):
    scores(0, s_a)

    def pair_body(i, c):
        scores(2 * i + 1, s_b)
        consume(2 * i, s_a, False)
        scores(2 * i + 2, s_a)
        consume(2 * i + 1, s_b, False)
        return c

    lax.fori_loop(0, n_full // 2, pair_body, 0)

    @pl.when(n_full % 2 == 0)
    def _():
        consume(n_full, s_a, True)

    @pl.when(n_full % 2 == 1)
    def _():
        scores(n_full, s_b)
        consume(n_full - 1, s_a, False)
        consume(n_full, s_b, True)


def _diff_lambda(lam_ref, lam_init):
    lf = lam_ref[...]
    a = jnp.sum(lf[0:1, :] * lf[1:2, :], axis=1, keepdims=True)
    b = jnp.sum(lf[2:3, :] * lf[3:4, :], axis=1, keepdims=True)
    return jnp.exp(a) - jnp.exp(b) + lam_init


def _diff_attn_body(qT_ref, kT_ref, v_ref, lam_ref, gsub_ref, o_ref, m_scr, acc_scr, sa_scr, sb_scr,
                    *, tile, lam_init):
    qi = pl.program_id(1)
    hw = 2 * HEAD_DIM
    q = qT_ref[...].T * (HEAD_DIM ** -0.5 * LOG2E)
    lane = lax.broadcasted_iota(jnp.int32, q.shape, 1)
    q12 = jnp.concatenate([jnp.where(lane < HEAD_DIM, q, 0.0), jnp.where(lane >= HEAD_DIM, q, 0.0)],
                          axis=0).astype(BF16)
    _init_flash_state(m_scr, acc_scr)
    ones = jnp.ones((tile, hw), BF16)

    def scores(kb, s_ref):
        off = pl.multiple_of(kb * tile, tile)
        s_ref[...] = jnp.dot(q12, kT_ref[:, pl.ds(off, tile)], preferred_element_type=F32)

    def consume(kb, s_ref, masked):
        off = pl.multiple_of(kb * tile, tile)
        v_aug = jnp.concatenate([v_ref[pl.ds(off, tile), :], ones], axis=1)
        s = s_ref[...]
        if masked:
            row = jnp.bitwise_and(lax.broadcasted_iota(jnp.int32, s.shape, 0), tile - 1)
            col = lax.broadcasted_iota(jnp.int32, s.shape, 1)
            s = jnp.where(col <= row, s, MASK_NEG)
        _flash_step(s, v_aug, m_scr, acc_scr)

    _run_pipelined(qi, scores, consume, sa_scr, sb_scr)

    lam = _diff_lambda(lam_ref, lam_init)
    acc = acc_scr[...]
    o = acc[:, :hw] / acc[:, hw:]
    o = o[:tile, :] - lam * o[tile:, :]
    o = _rms_rows(o, gsub_ref[...]) * (1.0 - lam_init)
    o_ref[...] = o.astype(o_ref.dtype)


def _diff_attn_prompt(qT, kTb, vb, lam, gsub, *, lam_init, tile):
    T = qT.shape[1]
    hw = 2 * HEAD_DIM
    assert T % tile == 0
    return pl.pallas_call(
        functools.partial(_diff_attn_body, tile=tile, lam_init=lam_init),
        grid=(N_HEADS_A, T // tile),
        in_specs=[
            pl.BlockSpec((hw, tile), lambda h, i: (h, i)),
            pl.BlockSpec((hw, T), lambda h, i: (h, 0)),
            pl.BlockSpec((T, hw), lambda h, i: (0, h)),
            pl.BlockSpec((4, HEAD_DIM), lambda h, i: (0, 0)),
            pl.BlockSpec((1, hw), lambda h, i: (0, 0)),
        ],
        out_specs=pl.BlockSpec((tile, hw), lambda h, i: (i, h)),
        out_shape=jax.ShapeDtypeStruct((T, D_MODEL), BF16),
        scratch_shapes=[pltpu.VMEM((2 * tile, LANES), F32), pltpu.VMEM((2 * tile, 2 * hw), F32),
                        pltpu.VMEM((2 * tile, tile), F32), pltpu.VMEM((2 * tile, tile), F32)],
        compiler_params=_cparams("arbitrary", "arbitrary"),
        name="diff_attn_prompt",
    )(qT, kTb, vb, lam, gsub)


def _topk_pick(g, posf, k, axis=1):
    picked = jnp.zeros(g.shape, jnp.bool_)
    idxs = []
    for _ in range(k):
        m = jnp.max(g, axis=axis, keepdims=True)
        idx = jnp.min(jnp.where(g == m, posf, float(4 * LANES)), axis=axis, keepdims=True)
        hit = posf == idx
        picked = jnp.logical_or(picked, hit)
        g = jnp.where(hit, KNOCKED, g)
        idxs.append(idx)
    return picked, idxs


def _moba_body(qT_ref, kT_ref, v_ref, km_ref, o_ref, ma, aa, mb, ab, sa_scr, sb_scr, *, tile):
    qi = pl.program_id(1)
    blk_shift = MOBA_BLOCK.bit_length() - 1
    bpt = tile // MOBA_BLOCK
    qT = qT_ref[...]
    qsT = qT * (HEAD_DIM ** -0.5 * LOG2E)
    lo = lax.broadcasted_iota(jnp.int32, (tile, 2 * HEAD_DIM), 1) < HEAD_DIM
    blk = lax.broadcasted_iota(jnp.int32, (HEAD_DIM, tile), 0)
    blkf = blk.astype(F32)
    own = qi * bpt + jnp.right_shift(lax.broadcasted_iota(jnp.int32, (HEAD_DIM, tile), 1), blk_shift)
    past = blk < own

    def select_bias(hh):
        g = jnp.dot(km_ref[hh], qT[hh * HEAD_DIM:(hh + 1) * HEAD_DIM, :], precision=HIGHEST,
                    preferred_element_type=F32)
        g = jnp.where(past, g, MASK_NEG)
        picked, _ = _topk_pick(g, blkf, MOBA_TOPK, axis=0)
        sel = jnp.logical_or(jnp.logical_and(picked, past), blk == own)
        return jnp.where(sel, 0.0, -SEL_BIAS)

    qa = jnp.concatenate([qsT[:HEAD_DIM, :], select_bias(0)], axis=0).T
    qb = jnp.concatenate([select_bias(1), qsT[HEAD_DIM:, :]], axis=0).T
    zq = jnp.zeros_like(qa)
    qc = jnp.concatenate([jnp.concatenate([qa, zq], axis=1), jnp.concatenate([zq, qb], axis=1)],
                         axis=0).astype(BF16)
    _init_flash_state(ma, aa)
    _init_flash_state(mb, ab)
    sub = lax.broadcasted_iota(jnp.int32, (HEAD_DIM, tile), 0)
    key_blk = jnp.right_shift(lax.broadcasted_iota(jnp.int32, (HEAD_DIM, tile), 1), blk_shift)
    vlane = lax.broadcasted_iota(jnp.int32, (tile, 2 * HEAD_DIM), 1)
    one = jnp.ones((tile, 2 * HEAD_DIM), BF16)

    def scores(kb, s_ref):
        off = pl.multiple_of(kb * tile, tile)
        kt = kT_ref[:, pl.ds(off, tile)]
        onehot = jnp.where(sub == kb * bpt + key_blk, 1.0, 0.0).astype(BF16)
        kc = jnp.concatenate([kt[:HEAD_DIM, :], onehot, onehot, kt[HEAD_DIM:, :]], axis=0)
        s_ref[...] = jnp.dot(qc, kc, preferred_element_type=F32)

    def consume(kb, s_ref, masked):
        off = pl.multiple_of(kb * tile, tile)
        vv = v_ref[pl.ds(off, tile), :]
        sa = s_ref[:tile, :]
        sb = s_ref[tile:, :]
        if masked:
            row = lax.broadcasted_iota(jnp.int32, (tile, tile), 0)
            col = lax.broadcasted_iota(jnp.int32, (tile, tile), 1)
            keep = col <= row
            sa = jnp.where(keep, sa, MASK_NEG)
            sb = jnp.where(keep, sb, MASK_NEG)
        _flash_step(sa, jnp.where(vlane < HEAD_DIM, vv, one), ma, aa)
        _flash_step(sb, jnp.where(vlane >= HEAD_DIM, vv, one), mb, ab)

    _run_pipelined(qi, scores, consume, sa_scr, sb_scr)
    acc_a = aa[...]
    acc_b = ab[...]
    oa = acc_a / pltpu.roll(acc_a, HEAD_DIM, 1)
    ob = acc_b / pltpu.roll(acc_b, HEAD_DIM, 1)
    o_ref[...] = jnp.where(lo, oa, ob).astype(o_ref.dtype)


def _moba_prompt(qT, kTb, vb, km_aug, *, tile):
    T = qT.shape[1]
    hw = 2 * HEAD_DIM
    assert T % tile == 0 and tile % MOBA_BLOCK == 0 and T // MOBA_BLOCK <= HEAD_DIM
    return pl.pallas_call(
        functools.partial(_moba_body, tile=tile),
        grid=(N_HEADS_B // 2, T // tile),
        in_specs=[
            pl.BlockSpec((hw, tile), lambda h, i: (h, i)),
            pl.BlockSpec((hw, T), lambda h, i: (h, 0)),
            pl.BlockSpec((T, hw), lambda h, i: (0, h)),
            pl.BlockSpec((2, HEAD_DIM, HEAD_DIM), lambda h, i: (h, 0, 0)),
        ],
        out_specs=pl.BlockSpec((tile, hw), lambda h, i: (i, h)),
        out_shape=jax.ShapeDtypeStruct((T, D_MODEL), BF16),
        scratch_shapes=[
            pltpu.VMEM((tile, LANES), F32), pltpu.VMEM((tile, hw), F32),
            pltpu.VMEM((tile, LANES), F32), pltpu.VMEM((tile, hw), F32),
            pltpu.VMEM((2 * tile, tile), F32), pltpu.VMEM((2 * tile, tile), F32),
        ],
        compiler_params=_cparams("arbitrary", "arbitrary"),
        name="moba_attn_prompt",
    )(qT, kTb, vb, km_aug)


def _moba_kmean_per_head(kmean):
    n_blk = kmean.shape[0]
    km = kmean.reshape(n_blk, N_HEADS_B, HEAD_DIM).transpose(1, 0, 2)
    return jnp.pad(km, ((0, 0), (0, HEAD_DIM - n_blk), (0, 0)))


def _oproj_body(x_ref, o_ref, w_ref, y_ref):
    y_ref[...] = x_ref[...] + jnp.dot(o_ref[...], w_ref[...], preferred_element_type=F32)


def _oproj(x, o, w, *, tile):
    T = x.shape[0]
    rowblk = pl.BlockSpec((tile, D_MODEL), lambda i: (i, 0))
    return pl.pallas_call(
        _oproj_body,
        grid=(T // tile,),
        in_specs=[rowblk, rowblk, pl.BlockSpec((D_MODEL, D_MODEL), lambda i: (0, 0))],
        out_specs=rowblk,
        out_shape=jax.ShapeDtypeStruct((T, D_MODEL), F32),
        compiler_params=_cparams("arbitrary"),
        name="attn_out_proj",
    )(x, o, w)


def _swiglu_chunk(hn, wg, wu, wd):
    a = jnp.dot(hn, wg, preferred_element_type=F32)
    u = jnp.dot(hn, wu, preferred_element_type=F32)
    act = (a * _sigmoid(a) * u).astype(BF16)
    return jnp.dot(act, wd, preferred_element_type=F32)


def _ffn_body(x_ref, g_ref, wg_ref, wu_ref, wd_ref, y_ref, hn_scr, acc_scr):
    f = pl.program_id(1)

    @pl.when(f == 0)
    def _():
        x = x_ref[...]
        hn_scr[...] = _rms_rows(x, g_ref[...]).astype(BF16)
        acc_scr[...] = x

    acc_scr[...] += _swiglu_chunk(hn_scr[...], wg_ref[...], wu_ref[...], wd_ref[...])

    @pl.when(f == pl.num_programs(1) - 1)
    def _():
        y_ref[...] = acc_scr[...]


def _ffn(x, g, wg, wu, wd, *, tile, tf):
    T = x.shape[0]
    F = wg.shape[1]
    assert F % tf == 0
    rowblk = pl.BlockSpec((tile, D_MODEL), lambda i, f: (i, 0))
    return pl.pallas_call(
        _ffn_body,
        grid=(T // tile, F // tf),
        in_specs=[
            rowblk,
            pl.BlockSpec((1, D_MODEL), lambda i, f: (0, 0)),
            pl.BlockSpec((D_MODEL, tf), lambda i, f: (0, f)),
            pl.BlockSpec((D_MODEL, tf), lambda i, f: (0, f)),
            pl.BlockSpec((tf, D_MODEL), lambda i, f: (f, 0)),
        ],
        out_specs=rowblk,
        out_shape=jax.ShapeDtypeStruct((T, D_MODEL), F32),
        scratch_shapes=[pltpu.VMEM((tile, D_MODEL), BF16), pltpu.VMEM((tile, D_MODEL), F32)],
        compiler_params=_cparams("arbitrary", "arbitrary"),
        name="dense_swiglu",
    )(x, g, wg, wu, wd)


ROUTE_MASK = 8
ROUTE_RANK = 16


def _router_body(x_ref, g_ref, wr_ref, hn_ref, route_ref, routeT_ref, count_ref):
    hf = _rms_rows(x_ref[...], g_ref[...])
    hn_ref[...] = hf.astype(BF16)
    logits = jnp.dot(hf, wr_ref[...], precision=HIGHEST, preferred_element_type=F32)
    lane = lax.broadcasted_iota(jnp.int32, logits.shape, 1)
    lanef = lane.astype(F32)
    logits = jnp.where(lane < N_EXPERTS, logits, MASK_NEG)
    v1 = jnp.max(logits, axis=1, keepdims=True)
    i1 = jnp.min(jnp.where(logits == v1, lanef, float(LANES)), axis=1, keepdims=True)
    rest = jnp.where(lanef == i1, KNOCKED, logits)
    v2 = jnp.max(rest, axis=1, keepdims=True)
    i2 = jnp.min(jnp.where(rest == v2, lanef, float(LANES)), axis=1, keepdims=True)
    e2 = jnp.exp(v2 - v1)
    g1 = 1.0 / (1.0 + e2)
    g2 = e2 / (1.0 + e2)
    weights = jnp.where(lanef == i1, g1, 0.0) + jnp.where(lanef == i2, g2, 0.0)
    le = lanef - float(ROUTE_RANK)
    routed = jnp.where(jnp.logical_or(le == i1, le == i2), 1.0, 0.0)
    t = hf.shape[0]
    before = (lax.broadcasted_iota(jnp.int32, (t, t), 1) < lax.broadcasted_iota(jnp.int32, (t, t), 0))
    rank = jnp.dot(jnp.where(before, 1.0, 0.0).astype(BF16), routed.astype(BF16),
                   preferred_element_type=F32)
    route = weights + pltpu.roll(routed, LANES - (ROUTE_RANK - ROUTE_MASK), 1) + rank * routed
    route_ref[...] = route
    eye = jnp.where(lax.broadcasted_iota(jnp.int32, (LANES, LANES), 0)
                    == lax.broadcasted_iota(jnp.int32, (LANES, LANES), 1), 1.0, 0.0)
    routeT_ref[...] = lax.dot_general(eye, route, NT_DIMS, precision=HIGHEST,
                                      preferred_element_type=F32)
    count_ref[0] = jnp.sum(routed, axis=0, keepdims=True).astype(jnp.int32)


def _router(x, g, wr, *, tile):
    T = x.shape[0]
    n = T // tile
    rowblk = pl.BlockSpec((tile, D_MODEL), lambda i: (i, 0))
    return pl.pallas_call(
        _router_body,
        grid=(n,),
        in_specs=[rowblk, pl.BlockSpec((1, D_MODEL), lambda i: (0, 0)),
                  pl.BlockSpec((D_MODEL, LANES), lambda i: (0, 0))],
        out_specs=[rowblk, pl.BlockSpec((tile, LANES), lambda i: (i, 0)),
                   pl.BlockSpec((LANES, tile), lambda i: (0, i)),
                   pl.BlockSpec((1, 1, LANES), lambda i: (i, 0, 0))],
        out_shape=[jax.ShapeDtypeStruct((T, D_MODEL), BF16), jax.ShapeDtypeStruct((T, LANES), F32),
                   jax.ShapeDtypeStruct((LANES, T), F32), jax.ShapeDtypeStruct((n, 1, LANES), jnp.int32)],
        compiler_params=_cparams("arbitrary"),
        name="moe_router",
    )(x, g, wr)


def _experts_body(cnt_ref, hn_ref, route_ref, routeT_ref, wg_ref, wu_ref, wd_ref, y_ref,
                  x_scr, y_scr, *, cap):
    i = pl.program_id(0)
    e = pl.program_id(1)
    f = pl.program_id(2)
    last_f = f == pl.num_programs(2) - 1
    t = hn_ref.shape[0]
    n_chunks = lax.div(cnt_ref[i, e] + (cap - 1), cap)

    @pl.when(jnp.logical_and(e == 0, f == 0))
    def _():
        y_ref[...] = jnp.zeros(y_ref.shape, F32)

    @pl.when(f == 0)
    def _():
        rankT = routeT_ref[pl.ds(ROUTE_RANK + e, 1), :]
        maskT = routeT_ref[pl.ds(ROUTE_MASK + e, 1), :]
        rankT = jnp.where(maskT > 0.5, rankT, -1.0)
        slot = lax.broadcasted_iota(jnp.int32, (cap, t), 0).astype(F32)

        def gather(c, carry):
            r0 = pl.multiple_of(c * cap, 16)
            onehot = jnp.where(rankT - (c * cap).astype(F32) == slot, 1.0, 0.0).astype(BF16)
            x_scr[pl.ds(r0, cap), :] = jnp.dot(onehot, hn_ref[...],
                                               preferred_element_type=F32).astype(BF16)
            y_scr[pl.ds(r0, cap), :] = jnp.zeros((cap, D_MODEL), F32)
            return carry

        lax.fori_loop(0, n_chunks, gather, 0)

    def expert(c, carry):
        r0 = pl.multiple_of(c * cap, 16)
        y_scr[pl.ds(r0, cap), :] += _swiglu_chunk(x_scr[pl.ds(r0, cap), :], wg_ref[...], wu_ref[...],
                                                  wd_ref[...])
        return carry

    lax.fori_loop(0, n_chunks, expert, 0)

    @pl.when(last_f)
    def _():
        route = route_ref[...]
        lane = lax.broadcasted_iota(jnp.int32, route.shape, 1)
        pick = lambda off: jnp.sum(jnp.where(lane == off + e, route, 0.0), axis=1, keepdims=True)
        w_e = pick(0)
        rank = jnp.where(pick(ROUTE_MASK) > 0.5, pick(ROUTE_RANK), -1.0)
        slot = lax.broadcasted_iota(jnp.int32, (t, cap), 1).astype(F32)

        def scatter(c, carry):
            r0 = pl.multiple_of(c * cap, 16)
            onehot = jnp.where(rank - (c * cap).astype(F32) == slot, 1.0, 0.0).astype(BF16)
            y = y_scr[pl.ds(r0, cap), :]
            y_hi = y.astype(BF16)
            y_lo = (y - y_hi.astype(F32)).astype(BF16)
            back = (jnp.dot(onehot, y_hi, preferred_element_type=F32)
                    + jnp.dot(onehot, y_lo, preferred_element_type=F32))
            y_ref[...] += w_e * back
            return carry

        lax.fori_loop(0, n_chunks, scatter, 0)


def _experts(counts, hn, route, routeT, wg, wu, wd, *, tile, tf, cap):
    T = hn.shape[0]
    E, _, F = wg.shape
    assert F % tf == 0 and cap % 16 == 0
    rows = -(-tile // cap) * cap
    rowblk = pl.BlockSpec((tile, D_MODEL), lambda i, e, f, cnt: (i, 0))
    grid_spec = pltpu.PrefetchScalarGridSpec(
        num_scalar_prefetch=1,
        grid=(T // tile, E, F // tf),
        in_specs=[
            rowblk,
            pl.BlockSpec((tile, LANES), lambda i, e, f, cnt: (i, 0)),
            pl.BlockSpec((LANES, tile), lambda i, e, f, cnt: (0, i)),
            pl.BlockSpec((None, D_MODEL, tf), lambda i, e, f, cnt: (e, 0, f)),
            pl.BlockSpec((None, D_MODEL, tf), lambda i, e, f, cnt: (e, 0, f)),
            pl.BlockSpec((None, tf, D_MODEL), lambda i, e, f, cnt: (e, f, 0)),
        ],
        out_specs=rowblk,
        scratch_shapes=[pltpu.VMEM((rows, D_MODEL), BF16), pltpu.VMEM((rows, D_MODEL), F32)],
    )
    return pl.pallas_call(
        functools.partial(_experts_body, cap=cap),
        grid_spec=grid_spec,
        out_shape=jax.ShapeDtypeStruct((T, D_MODEL), F32),
        compiler_params=_cparams("arbitrary", "arbitrary", "arbitrary"),
        name="moe_experts",
    )(counts, hn, route, routeT, wg, wu, wd)


def _ple_body(x_ref, *rest, with_addend):
    if with_addend:
        add_ref, g_ref, wgate_ref, p_ref, win_ref, y_ref = rest
        x = x_ref[...] + add_ref[...]
    else:
        g_ref, wgate_ref, p_ref, win_ref, y_ref = rest
        x = x_ref[...]
    hn = _rms_rows(x, g_ref[...]).astype(BF16)
    gate = _sigmoid(jnp.dot(hn, wgate_ref[...], preferred_element_type=F32))
    emb = jnp.dot(p_ref[...].astype(BF16), win_ref[...], preferred_element_type=F32)
    y_ref[...] = x + gate * emb


def _ple(x, addend, g, wgate, p, win, *, tile):
    T = x.shape[0]
    pd = p.shape[1]
    rowblk = pl.BlockSpec((tile, D_MODEL), lambda i: (i, 0))
    acts = (x,) if addend is None else (x, addend)
    return pl.pallas_call(
        functools.partial(_ple_body, with_addend=addend is not None),
        grid=(T // tile,),
        in_specs=[rowblk] * len(acts) + [
            pl.BlockSpec((1, D_MODEL), lambda i: (0, 0)),
            pl.BlockSpec((D_MODEL, D_MODEL), lambda i: (0, 0)),
            pl.BlockSpec((tile, pd), lambda i: (i, 0)),
            pl.BlockSpec((pd, D_MODEL), lambda i: (0, 0)),
        ],
        out_specs=rowblk,
        out_shape=jax.ShapeDtypeStruct((T, D_MODEL), F32),
        compiler_params=_cparams("arbitrary"),
        name="per_layer_embedding",
    )(*acts, g, wgate, p, win)


PAGES_PER_STEP = 8


def _diff_decode_body(pt_ref, qbd_ref, knew_ref, vnew_ref, lam_ref, gsub_ref, *rest,
                      dec_seq, lam_init):
    kT_refs = rest[:PAGES_PER_STEP]
    v_refs = rest[PAGES_PER_STEP:2 * PAGES_PER_STEP]
    o_ref, m_ref, l_ref, acc_ref = rest[2 * PAGES_PER_STEP:]
    j = pl.program_id(1)
    n_rows = qbd_ref.shape[0]

    @pl.when(j == 0)
    def _():
        _init_softmax_state(m_ref, l_ref, acc_ref)

    qbd = qbd_ref[...]
    s = jnp.concatenate(
        [jnp.dot(qbd, r[...].astype(BF16), preferred_element_type=F32) for r in kT_refs], axis=1)
    m_old = m_ref[...]
    m_new = jnp.maximum(m_old, jnp.max(s, axis=1, keepdims=True))
    p = jnp.exp(s - m_new)
    alpha = jnp.exp(m_old - m_new)
    l_ref[...] = alpha * l_ref[...] + jnp.sum(p, axis=1, keepdims=True)
    acc = alpha * acc_ref[...]
    for i, r in enumerate(v_refs):
        vcat = jnp.concatenate(
            [r[pl.ds(h, PAGE_SIZE, stride=N_HEADS_A), :] for h in range(N_HEADS_A)], axis=1)
        acc += jnp.dot(p[:, i * PAGE_SIZE:(i + 1) * PAGE_SIZE].astype(BF16), vcat.astype(BF16),
                       preferred_element_type=F32)
    acc_ref[...] = acc
    m_ref[...] = m_new

    @pl.when(j == pl.num_programs(1) - 1)
    def _():
        sn = lax.dot_general(qbd, knew_ref[...], NT_DIMS, preferred_element_type=F32)
        row = lax.broadcasted_iota(jnp.int32, sn.shape, 0)
        col = lax.broadcasted_iota(jnp.int32, sn.shape, 1)
        half = n_rows // 2
        t_of_row = jnp.right_shift(jnp.bitwise_and(row, half - 1), 3)
        sn = jnp.where(jnp.logical_and(col <= t_of_row, col < dec_seq), sn, MASK_NEG)
        _online_update(sn, vnew_ref[...], m_ref, l_ref, acc_ref)
        o = acc_ref[...] / l_ref[...]
        lam = _diff_lambda(lam_ref, lam_init)
        od = o[:half, :] - lam * o[half:, :]
        r2 = lax.broadcasted_iota(jnp.int32, od.shape, 0)
        c2 = lax.broadcasted_iota(jnp.int32, od.shape, 1)
        own = jnp.right_shift(c2, 7) == jnp.bitwise_and(r2, N_HEADS_A - 1)
        od = jnp.where(own, od, 0.0)
        ms = jnp.sum(od * od, axis=1, keepdims=True) / (2 * HEAD_DIM)
        od = od * lax.rsqrt(ms + EPS) * gsub_ref[...] * (1.0 - lam_init)
        rows = [jnp.sum(od[t * N_HEADS_A:(t + 1) * N_HEADS_A, :], axis=0, keepdims=True)
                for t in range(dec_seq)]
        o_ref[...] = jnp.concatenate(rows, axis=0)


def _diff_decode(page_table, qbd, knew, vnew, lam, gsub_tiled, kT_pages, v_rows, *, dec_seq,
                 lam_init):
    nb, n_pages = page_table.shape
    assert n_pages % PAGES_PER_STEP == 0
    n_rows = qbd.shape[1]
    kv_rows = PAGE_SIZE * N_HEADS_A

    def k_spec(i):
        return pl.BlockSpec((None, D_MODEL, PAGE_SIZE),
                            lambda b, j, pt: (pt[b, j * PAGES_PER_STEP + i], 0, 0))

    def v_spec(i):
        return pl.BlockSpec((kv_rows, 2 * HEAD_DIM),
                            lambda b, j, pt: (pt[b, j * PAGES_PER_STEP + i], 0))

    per_seq = lambda b, j, pt: (b, 0, 0)
    const2 = lambda b, j, pt: (0, 0)
    grid_spec = pltpu.PrefetchScalarGridSpec(
        num_scalar_prefetch=1,
        grid=(nb, n_pages // PAGES_PER_STEP),
        in_specs=[
            pl.BlockSpec((None, n_rows, D_MODEL), per_seq),
            pl.BlockSpec((None, PAGE_SIZE, D_MODEL), per_seq),
            pl.BlockSpec((None, PAGE_SIZE, D_MODEL), per_seq),
            pl.BlockSpec((4, HEAD_DIM), const2),
            pl.BlockSpec((1, D_MODEL), const2),
        ] + [k_spec(i) for i in range(PAGES_PER_STEP)] + [v_spec(i) for i in range(PAGES_PER_STEP)],
        out_specs=pl.BlockSpec((None, dec_seq, D_MODEL), per_seq),
        scratch_shapes=[pltpu.VMEM((n_rows, 1), F32), pltpu.VMEM((n_rows, 1), F32),
                        pltpu.VMEM((n_rows, D_MODEL), F32)],
    )
    return pl.pallas_call(
        functools.partial(_diff_decode_body, dec_seq=dec_seq, lam_init=lam_init),
        grid_spec=grid_spec,
        out_shape=jax.ShapeDtypeStruct((nb, dec_seq, D_MODEL), F32),
        compiler_params=_cparams("arbitrary", "arbitrary"),
        name="diff_attn_decode",
    )(page_table, qbd, knew, vnew, lam, gsub_tiled,
      *([kT_pages] * PAGES_PER_STEP), *([v_rows] * PAGES_PER_STEP))


def _moba_gate_body(pt_ref, qbd_ref, *rest):
    kT_refs = rest[:PAGES_PER_STEP]
    idx_ref, km_scr = rest[PAGES_PER_STEP:]
    j = pl.program_id(1)
    pages_per_block = MOBA_BLOCK // PAGE_SIZE
    blocks_per_step = PAGES_PER_STEP // pages_per_block

    @pl.when(j == 0)
    def _():
        km_scr[...] = jnp.zeros(km_scr.shape, F32)

    lane = lax.broadcasted_iota(jnp.int32, km_scr.shape, 1)
    km = km_scr[...]
    for bb in range(blocks_per_step):
        pages = kT_refs[bb * pages_per_block][...]
        for i in range(1, pages_per_block):
            pages = pages + kT_refs[bb * pages_per_block + i][...]
        tot = jnp.sum(pages, axis=1, keepdims=True)
        km = jnp.where(lane == j * blocks_per_step + bb, tot * (1.0 / MOBA_BLOCK), km)
    km_scr[...] = km

    @pl.when(j == pl.num_programs(1) - 1)
    def _():
        n_blk = pl.num_programs(1) * blocks_per_step
        g = jnp.dot(qbd_ref[...], km_scr[...], precision=HIGHEST, preferred_element_type=F32)
        glane = lax.broadcasted_iota(jnp.int32, g.shape, 1)
        glanef = glane.astype(F32)
        g = jnp.where(glane < n_blk, g, MASK_NEG)
        _, idxs = _topk_pick(g, glanef, MOBA_TOPK)
        out = jnp.zeros(g.shape, F32)
        for r, idx in enumerate(idxs):
            out = jnp.where(glane == r, idx, out)
        idx_ref[...] = out.astype(jnp.int32)


def _moba_gate(page_table, qbd, kT_pages):
    nb, n_pages = page_table.shape
    assert n_pages % PAGES_PER_STEP == 0 and n_pages * PAGE_SIZE // MOBA_BLOCK <= LANES
    n_rows = qbd.shape[1]

    def k_spec(i):
        return pl.BlockSpec((None, D_MODEL, PAGE_SIZE),
                            lambda b, j, pt: (pt[b, j * PAGES_PER_STEP + i], 0, 0))

    per_seq = lambda b, j, pt: (b, 0, 0)
    grid_spec = pltpu.PrefetchScalarGridSpec(
        num_scalar_prefetch=1,
        grid=(nb, n_pages // PAGES_PER_STEP),
        in_specs=[pl.BlockSpec((None, n_rows, D_MODEL), per_seq)]
        + [k_spec(i) for i in range(PAGES_PER_STEP)],
        out_specs=pl.BlockSpec((None, n_rows, LANES), per_seq),
        scratch_shapes=[pltpu.VMEM((D_MODEL, LANES), F32)],
    )
    return pl.pallas_call(
        _moba_gate_body,
        grid_spec=grid_spec,
        out_shape=jax.ShapeDtypeStruct((nb, n_rows, LANES), jnp.int32),
        compiler_params=_cparams("arbitrary", "arbitrary"),
        name="moba_decode_gate",
    )(page_table, qbd, *([kT_pages] * PAGES_PER_STEP))


def _moba_sel_body(phys_ref, q_ref, knew_ref, vnew_ref, *rest, dec_seq, n_sel):
    k_refs = rest[:dec_seq * n_sel]
    v_refs = rest[dec_seq * n_sel:2 * dec_seq * n_sel]
    (o_ref,) = rest[2 * dec_seq * n_sel:]
    q = q_ref[...]
    lane = lax.broadcasted_iota(jnp.int32, (1, PAGE_SIZE), 1)
    olane = lax.broadcasted_iota(jnp.int32, o_ref.shape, 1)
    out = jnp.zeros(o_ref.shape, F32)
    for t in range(dec_seq):
        qcol = q[:, t:t + 1]
        ks = [k_refs[t * n_sel + s][...] for s in range(n_sel)] + [knew_ref[...]]
        vs = [v_refs[t * n_sel + s][...] for s in range(n_sel)] + [vnew_ref[...]]
        scores = [jnp.sum(k * qcol, axis=0, keepdims=True) for k in ks]
        scores[-1] = jnp.where(lane <= t, scores[-1], MASK_NEG)
        m = scores[0].max(axis=1, keepdims=True)
        for sc in scores[1:]:
            m = jnp.maximum(m, sc.max(axis=1, keepdims=True))
        ps = [jnp.exp(sc - m) for sc in scores]
        l = ps[0].sum(axis=1, keepdims=True)
        for pp in ps[1:]:
            l = l + pp.sum(axis=1, keepdims=True)
        o = jnp.sum(vs[0] * ps[0], axis=1, keepdims=True)
        for vv, pp in zip(vs[1:], ps[1:]):
            o = o + jnp.sum(vv * pp, axis=1, keepdims=True)
        out = jnp.where(olane == t, o / l, out)
    o_ref[...] = out


def _moba_decode_attend(phys, qT, knewT, vnewT, kT_slices, vT_slices, *, dec_seq):
    nb, nh, n_slices = phys.shape
    n_sel = n_slices // dec_seq

    def sel_spec(s):
        return pl.BlockSpec((None, None, HEAD_DIM, PAGE_SIZE),
                            lambda b, h, ph: (ph[b, h, s], h, 0, 0))

    per_head = pl.BlockSpec((None, None, HEAD_DIM, PAGE_SIZE), lambda b, h, ph: (b, h, 0, 0))
    grid_spec = pltpu.PrefetchScalarGridSpec(
        num_scalar_prefetch=1,
        grid=(nb, nh),
        in_specs=[per_head, per_head, per_head]
        + [sel_spec(s) for s in range(n_slices)] + [sel_spec(s) for s in range(n_slices)],
        out_specs=per_head,
    )
    return pl.pallas_call(
        functools.partial(_moba_sel_body, dec_seq=dec_seq, n_sel=n_sel),
        grid_spec=grid_spec,
        out_shape=jax.ShapeDtypeStruct((nb, nh, HEAD_DIM, PAGE_SIZE), F32),
        compiler_params=_cparams("arbitrary", "arbitrary"),
        name="moba_decode_attend",
    )(phys, qT, knewT, vnewT, *([kT_slices] * n_slices), *([vT_slices] * n_slices))


def _rope_tables(pos):
    inv = ROPE_THETA ** (-jnp.arange(0, HEAD_DIM, 2, dtype=F32) / HEAD_DIM)
    ang = pos.astype(F32)[:, None] * inv[None, :]
    ang = jnp.concatenate([ang, ang], axis=-1)
    return jnp.cos(ang).T, jnp.sin(ang).T


def _col(v):
    return v.reshape(-1, 1)


def _row(v):
    return v.reshape(1, -1)


def _pad_lanes(a, n):
    return jnp.pad(a, [(0, 0)] * (a.ndim - 1) + [(0, n - a.shape[-1])])


def kernel(x_prompt, x_sample, cache_k_a, cache_v_a, cache_k_b, cache_v_b, page_table, p_prompt, p_sample,
           g_mix, g_ffn, g_ple, w_ple_gate, w_ple_in, w_qkv_a, g_q_a, g_k_a, lam_a, g_sub_a, w_o_a,
           w_qkv_b, g_q_b, g_k_b, w_o_b, w_ff_gate, w_ff_up, w_ff_down, w_router, w_ex_gate, w_ex_up, w_ex_down):
    _, T, D = x_prompt.shape
    nb, dec_seq, _ = x_sample.shape
    n_dec = nb * dec_seq
    n_pool = cache_k_a.shape[1]
    past_len = page_table.shape[1] * PAGE_SIZE
    assert D == D_MODEL and n_dec % 8 == 0 and dec_seq <= PAGE_SIZE
    pt = page_table.astype(jnp.int32)

    tile_p = 512 if T % 512 == 0 else MOBA_BLOCK
    xp = x_prompt.reshape(T, D)
    xs = x_sample.reshape(n_dec, D)
    cos_p, sin_p = _rope_tables(jnp.arange(T))
    pos_s = past_len + jnp.tile(jnp.arange(dec_seq), nb)
    cos_s, sin_s = _rope_tables(pos_s)

    lam_init = 0.8 - 0.6 * math.exp(-0.3 * 0)
    wqkT = w_qkv_a[0][:, :2 * D].T.astype(BF16)
    wv = w_qkv_a[0][:, 2 * D:].astype(BF16)
    wo = w_o_a[0].astype(BF16)
    qkv_args = (_row(g_mix[0]), wqkT, wv, _col(g_q_a[0]), _col(g_k_a[0]))
    qT_p, kT_p, kTb_p, v_p, vb_p = _qkv(xp, *qkv_args, cos_p, sin_p, tile=MOBA_BLOCK, with_kmean=False)
    qT_s, kT_s, _, v_s, _ = _qkv(xs, *qkv_args, cos_s, sin_s, tile=n_dec, with_kmean=False)
    ka_p = kT_p.reshape(1, 1, N_HEADS_A, 2, HEAD_DIM, T).transpose(0, 1, 5, 2, 3, 4)
    va_p = v_p.reshape(1, 1, T, N_HEADS_A, 2 * HEAD_DIM)
    ka_s = kT_s.T.reshape(1, nb, dec_seq, N_HEADS_A, 2, HEAD_DIM)
    va_s = v_s.reshape(1, nb, dec_seq, N_HEADS_A, 2 * HEAD_DIM)

    o_p = _diff_attn_prompt(qT_p, kTb_p, vb_p, lam_a[0], _row(g_sub_a[0]), lam_init=lam_init,
                            tile=tile_p)

    q5 = (qT_s.T * (HEAD_DIM ** -0.5)).reshape(nb, dec_seq, N_HEADS_A, 2, HEAD_DIM)
    eye_h = jnp.eye(N_HEADS_A, dtype=F32)
    eye_c = jnp.eye(2, dtype=F32)
    qbd = jnp.einsum('bthcd,hH,cC->bcthHCd', q5, eye_h, eye_c).reshape(nb, 2 * dec_seq * N_HEADS_A, D)
    qbd = qbd.astype(BF16)
    knew = jnp.pad(kT_s.T.reshape(nb, dec_seq, D), ((0, 0), (0, PAGE_SIZE - dec_seq), (0, 0))).astype(BF16)
    vnew = jnp.pad(v_s.reshape(nb, dec_seq, D), ((0, 0), (0, PAGE_SIZE - dec_seq), (0, 0))).astype(BF16)
    kT_pages_a = cache_k_a[0].transpose(0, 2, 3, 4, 1).reshape(n_pool, D, PAGE_SIZE)
    v_rows_a = cache_v_a[0].reshape(n_pool * PAGE_SIZE * N_HEADS_A, 2 * HEAD_DIM)
    o_s = _diff_decode(pt, qbd, knew, vnew, lam_a[0], _row(jnp.tile(g_sub_a[0], N_HEADS_A)),
                       kT_pages_a, v_rows_a, dec_seq=dec_seq, lam_init=lam_init)
    o_s = o_s.reshape(n_dec, D).astype(BF16)

    xp = _oproj(xp, o_p, wo, tile=tile_p)
    xs = _oproj(xs, o_s, wo, tile=n_dec)

    wg = w_ff_gate[0].astype(BF16)
    wu = w_ff_up[0].astype(BF16)
    wd = w_ff_down[0].astype(BF16)
    tf = wg.shape[1] // 2
    xp = _ffn(xp, _row(g_ffn[0]), wg, wu, wd, tile=tile_p, tf=tf)
    xs = _ffn(xs, _row(g_ffn[0]), wg, wu, wd, tile=n_dec, tf=tf)

    wpg = w_ple_gate[0].astype(BF16)
    wpi = w_ple_in[0].astype(BF16)
    xp = _ple(xp, None, _row(g_ple[0]), wpg, p_prompt[0].reshape(T, -1), wpi, tile=tile_p)
    xs = _ple(xs, None, _row(g_ple[0]), wpg, p_sample[0].reshape(n_dec, -1), wpi, tile=n_dec)

    wqkT = w_qkv_b[0][:, :2 * D].T.astype(BF16)
    wv = w_qkv_b[0][:, 2 * D:].astype(BF16)
    wo = w_o_b[0].astype(BF16)
    qkv_args = (_row(g_mix[1]), wqkT, wv, _col(g_q_b[0]), _col(g_k_b[0]))
    qT_p, kT_p, kTb_p, v_p, vb_p, kmean = _qkv(xp, *qkv_args, cos_p, sin_p, tile=MOBA_BLOCK, with_kmean=True)
    qT_s, kT_s, _, v_s, _ = _qkv(xs, *qkv_args, cos_s, sin_s, tile=n_dec, with_kmean=False)
    kb_p = kT_p.reshape(1, 1, N_HEADS_B, HEAD_DIM, T).transpose(0, 1, 4, 2, 3)
    vb_out_p = v_p.reshape(1, 1, T, N_HEADS_B, HEAD_DIM)
    kb_s = kT_s.T.reshape(1, nb, dec_seq, N_HEADS_B, HEAD_DIM)
    vb_out_s = v_s.reshape(1, nb, dec_seq, N_HEADS_B, HEAD_DIM)

    o_p = _moba_prompt(qT_p, kTb_p, vb_p, _moba_kmean_per_head(kmean), tile=tile_p)

    q4 = qT_s.T.reshape(nb, dec_seq, N_HEADS_B, HEAD_DIM)
    qbd2 = jnp.einsum('bthd,hH->bhtHd', q4, jnp.eye(N_HEADS_B, dtype=F32)).reshape(nb, N_HEADS_B * dec_seq, D)
    kT_pages_b = cache_k_b[0].transpose(0, 2, 3, 1).reshape(n_pool, D, PAGE_SIZE)
    idx = _moba_gate(pt, qbd2, kT_pages_b)[:, :, :MOBA_TOPK]
    idx = idx.reshape(nb, N_HEADS_B, dec_seq, MOBA_TOPK)
    ppb = MOBA_BLOCK // PAGE_SIZE
    logical = (idx[..., None] * ppb + jnp.arange(ppb)).reshape(nb, N_HEADS_B, -1)
    phys = pt[jnp.arange(nb)[:, None, None], logical]
    per_head = lambda aT: aT.reshape(N_HEADS_B, HEAD_DIM, nb, dec_seq).transpose(2, 0, 1, 3)
    qT_h = _pad_lanes(per_head(qT_s) * (HEAD_DIM ** -0.5), PAGE_SIZE)
    knewT = _pad_lanes(per_head(kT_s), PAGE_SIZE)
    vnewT = _pad_lanes(per_head(v_s.T), PAGE_SIZE)
    kT_slices_b = cache_k_b[0].transpose(0, 2, 3, 1)
    vT_slices_b = cache_v_b[0].transpose(0, 2, 3, 1)
    oT = _moba_decode_attend(phys, qT_h, knewT, vnewT, kT_slices_b, vT_slices_b, dec_seq=dec_seq)
    o_s = oT[..., :dec_seq].transpose(0, 3, 1, 2).reshape(n_dec, D).astype(BF16)

    xp = _oproj(xp, o_p, wo, tile=tile_p)
    xs = _oproj(xs, o_s, wo, tile=n_dec)

    wr = _pad_lanes(w_router[0], LANES)
    weg = w_ex_gate[0].astype(BF16)
    weu = w_ex_up[0].astype(BF16)
    wed = w_ex_down[0].astype(BF16)
    tfe = weg.shape[2] // 4
    def moe(x, tile, cap):
        hn, route, routeT, cnt = _router(x, _row(g_ffn[1]), wr, tile=tile)
        counts = cnt[:, 0, ROUTE_RANK:ROUTE_RANK + N_EXPERTS]
        return _experts(counts, hn, route, routeT, weg, weu, wed, tile=tile, tf=tfe, cap=cap)

    tile_e = 1024 if T % 1024 == 0 else tile_p
    moe_p = moe(xp, tile_e, (5 * tile_e // 16 + 15) // 16 * 16)
    moe_s = moe(xs, n_dec, (n_dec // 2 + 15) // 16 * 16)

    wpg = w_ple_gate[1].astype(BF16)
    wpi = w_ple_in[1].astype(BF16)
    xp = _ple(xp, moe_p, _row(g_ple[1]), wpg, p_prompt[1].reshape(T, -1), wpi, tile=tile_p)
    xs = _ple(xs, moe_s, _row(g_ple[1]), wpg, p_sample[1].reshape(n_dec, -1), wpi, tile=n_dec)

    return (xp.reshape(1, T, D), xs.reshape(nb, dec_seq, D), ka_p, va_p, kb_p, vb_out_p,
            ka_s, va_s, kb_s, vb_out_s)
```

```python
import functools
import math

import jax
import jax.numpy as jnp
from jax import lax
from jax.experimental import pallas as pl
from jax.experimental.pallas import tpu as pltpu

F32 = jnp.float32
BF16 = jnp.bfloat16
HIGHEST = lax.Precision.HIGHEST

D_MODEL = 1024
HEAD_DIM = 64
N_HEADS_A = 8
N_HEADS_B = 16
PAGE_SIZE = 128
MOBA_BLOCK = 256
MOBA_TOPK = 3
ROPE_THETA = 10000.0
N_EXPERTS = 8
EPS = 1e-6
LANES = 128
MASK_NEG = -1e30
KNOCKED = -3e38
SEL_BIAS = float(2 ** 30)
VMEM_LIMIT = 56 * 1024 * 1024

NT_DIMS = (((1,), (1,)), ((), ()))


def _cparams(*sem):
    return pltpu.CompilerParams(dimension_semantics=sem, vmem_limit_bytes=VMEM_LIMIT)


def _rms_rows(x, g):
    ms = jnp.mean(x * x, axis=-1, keepdims=True)
    return x * lax.rsqrt(ms + EPS) * g


def _sigmoid(x):
    return 1.0 / (1.0 + jnp.exp(-x))


def _qkv_body(x_ref, g_ref, wqk_ref, wv_ref, gq_ref, gk_ref, cos_ref, sin_ref,
              qT_ref, kT_ref, kTb_ref, v_ref, vb_ref, *rest, with_kmean):
    if with_kmean:
        kmean_ref, y_scr = rest
    else:
        (y_scr,) = rest
    hn = _rms_rows(x_ref[...], g_ref[...]).astype(BF16)
    v = jnp.dot(hn, wv_ref[...], preferred_element_type=F32)
    v_ref[...] = v
    vb_ref[...] = v.astype(BF16)
    y_scr[...] = lax.dot_general(wqk_ref[...], hn, NT_DIMS, preferred_element_type=F32)
    cos = cos_ref[...]
    sin = sin_ref[...]
    half = HEAD_DIM // 2
    n_groups = D_MODEL // HEAD_DIM
    for grp in range(2 * n_groups):
        y = y_scr[grp * HEAD_DIM:(grp + 1) * HEAD_DIM, :]
        gain = gq_ref[...] if grp < n_groups else gk_ref[...]
        ms = jnp.mean(y * y, axis=0, keepdims=True)
        y = y * lax.rsqrt(ms + EPS) * gain
        rot = jnp.concatenate([-y[half:, :], y[:half, :]], axis=0)
        y = y * cos + rot * sin
        if grp < n_groups:
            qT_ref[grp * HEAD_DIM:(grp + 1) * HEAD_DIM, :] = y
        else:
            r = (grp - n_groups) * HEAD_DIM
            kT_ref[r:r + HEAD_DIM, :] = y
            kTb_ref[r:r + HEAD_DIM, :] = y.astype(BF16)
    if with_kmean:
        t = x_ref.shape[0]
        avg = jnp.full((8, t), 1.0 / t, F32)
        km = lax.dot_general(avg, kT_ref[...], NT_DIMS, precision=HIGHEST,
                             preferred_element_type=F32)
        kmean_ref[0] = km[0:1, :]


def _qkv(x, g, wqkT, wv, gq, gk, cosT, sinT, *, tile, with_kmean):
    T = x.shape[0]
    assert T % tile == 0
    n = T // tile
    out_shape = [
        jax.ShapeDtypeStruct((D_MODEL, T), F32),
        jax.ShapeDtypeStruct((D_MODEL, T), F32),
        jax.ShapeDtypeStruct((D_MODEL, T), BF16),
        jax.ShapeDtypeStruct((T, D_MODEL), F32),
        jax.ShapeDtypeStruct((T, D_MODEL), BF16),
    ]
    colblk = pl.BlockSpec((D_MODEL, tile), lambda i: (0, i))
    rowblk = pl.BlockSpec((tile, D_MODEL), lambda i: (i, 0))
    out_specs = [colblk, colblk, colblk, rowblk, rowblk]
    if with_kmean:
        assert tile == MOBA_BLOCK
        out_shape.append(jax.ShapeDtypeStruct((n, 1, D_MODEL), F32))
        out_specs.append(pl.BlockSpec((1, 1, D_MODEL), lambda i: (i, 0, 0)))
    const2 = lambda i: (0, 0)
    return pl.pallas_call(
        functools.partial(_qkv_body, with_kmean=with_kmean),
        grid=(n,),
        in_specs=[
            rowblk,
            pl.BlockSpec((1, D_MODEL), const2),
            pl.BlockSpec((2 * D_MODEL, D_MODEL), const2),
            pl.BlockSpec((D_MODEL, D_MODEL), const2),
            pl.BlockSpec((HEAD_DIM, 1), const2),
            pl.BlockSpec((HEAD_DIM, 1), const2),
            pl.BlockSpec((HEAD_DIM, tile), lambda i: (0, i)),
            pl.BlockSpec((HEAD_DIM, tile), lambda i: (0, i)),
        ],
        out_specs=out_specs,
        out_shape=out_shape,
        scratch_shapes=[pltpu.VMEM((2 * D_MODEL, tile), F32)],
        compiler_params=_cparams("arbitrary"),
        name="qkv_norm_rope",
    )(x, g, wqkT, wv, gq, gk, cosT, sinT)


def _online_update(s, vv, m_ref, l_ref, acc_ref):
    m_old = m_ref[...]
    m_new = jnp.maximum(m_old, jnp.max(s, axis=1, keepdims=True))
    p = jnp.exp(s - m_new)
    alpha = jnp.exp(m_old - m_new)
    l_ref[...] = alpha * l_ref[...] + jnp.sum(p, axis=1, keepdims=True)
    acc_ref[...] = alpha * acc_ref[...] + jnp.dot(p.astype(BF16), vv, preferred_element_type=F32)
    m_ref[...] = m_new


def _init_softmax_state(m_ref, l_ref, acc_ref):
    m_ref[...] = jnp.full(m_ref.shape, MASK_NEG, F32)
    l_ref[...] = jnp.zeros(l_ref.shape, F32)
    acc_ref[...] = jnp.zeros(acc_ref.shape, F32)


LOG2E = math.log2(math.e)


def _flash_step(s, v_aug, m_ref, acc_ref):
    tk = s.shape[1]
    w = acc_ref.shape[1]
    m_old = m_ref[...]
    m_new = jnp.maximum(m_old, jnp.max(s, axis=1, keepdims=True))
    alpha = jnp.exp2(m_old - m_new)
    p = jnp.exp2(s - jnp.concatenate([m_new] * (tk // LANES), axis=1))
    pv = jnp.dot(p.astype(BF16), v_aug, preferred_element_type=F32)
    if w > LANES:
        alpha = jnp.concatenate([alpha] * (w // LANES), axis=1)
    acc_ref[...] = alpha * acc_ref[...] + pv
    m_ref[...] = m_new


def _init_flash_state(m_ref, acc_ref):
    m_ref[...] = jnp.full(m_ref.shape, MASK_NEG, F32)
    acc_ref[...] = jnp.zeros(acc_ref.shape, F32)


def _run_pipelined(n_full, scores, consume, s_a, s_b):
    scores(0, s_a)

    def pair_body(i, c):
        scores(2 * i + 1, s_b)
        consume(2 * i, s_a, False)
        scores(2 * i + 2, s_a)
        consume(2 * i + 1, s_b, False)
        return c

    lax.fori_loop(0, n_full // 2, pair_body, 0)

    @pl.when(n_full % 2 == 0)
    def _():
        consume(n_full, s_a, True)

    @pl.when(n_full % 2 == 1)
    def _():
        scores(n_full, s_b)
        consume(n_full - 1, s_a, False)
        consume(n_full, s_b, True)


def _diff_lambda(lam_ref, lam_init):
    lf = lam_ref[...]
    a = jnp.sum(lf[0:1, :] * lf[1:2, :], axis=1, keepdims=True)
    b = jnp.sum(lf[2:3, :] * lf[3:4, :], axis=1, keepdims=True)
    return jnp.exp(a) - jnp.exp(b) + lam_init


def _diff_attn_body(qT_ref, kT_ref, v_ref, lam_ref, gsub_ref, o_ref, m_scr, acc_scr, sa_scr, sb_scr,
                    *, tile, lam_init):
    qi = pl.program_id(1)
    hw = 2 * HEAD_DIM
    q = qT_ref[...].T * (HEAD_DIM ** -0.5 * LOG2E)
    lane = lax.broadcasted_iota(jnp.int32, q.shape, 1)
    q12 = jnp.concatenate([jnp.where(lane < HEAD_DIM, q, 0.0), jnp.where(lane >= HEAD_DIM, q, 0.0)],
                          axis=0).astype(BF16)
    _init_flash_state(m_scr, acc_scr)
    ones = jnp.ones((tile, hw), BF16)

    def scores(kb, s_ref):
        off = pl.multiple_of(kb * tile, tile)
        s_ref[...] = jnp.dot(q12, kT_ref[:, pl.ds(off, tile)], preferred_element_type=F32)

    def consume(kb, s_ref, masked):
        off = pl.multiple_of(kb * tile, tile)
        v_aug = jnp.concatenate([v_ref[pl.ds(off, tile), :], ones], axis=1)
        s = s_ref[...]
        if masked:
            row = jnp.bitwise_and(lax.broadcasted_iota(jnp.int32, s.shape, 0), tile - 1)
            col = lax.broadcasted_iota(jnp.int32, s.shape, 1)
            s = jnp.where(col <= row, s, MASK_NEG)
        _flash_step(s, v_aug, m_scr, acc_scr)

    _run_pipelined(qi, scores, consume, sa_scr, sb_scr)

    lam = _diff_lambda(lam_ref, lam_init)
    acc = acc_scr[...]
    o = acc[:, :hw] / acc[:, hw:]
    o = o[:tile, :] - lam * o[tile:, :]
    o = _rms_rows(o, gsub_ref[...]) * (1.0 - lam_init)
    o_ref[...] = o.astype(o_ref.dtype)


def _diff_attn_prompt(qT, kTb, vb, lam, gsub, *, lam_init, tile):
    T = qT.shape[1]
    hw = 2 * HEAD_DIM
    assert T % tile == 0
    return pl.pallas_call(
        functools.partial(_diff_attn_body, tile=tile, lam_init=lam_init),
        grid=(N_HEADS_A, T // tile),
        in_specs=[
            pl.BlockSpec((hw, tile), lambda h, i: (h, i)),
            pl.BlockSpec((hw, T), lambda h, i: (h, 0)),
            pl.BlockSpec((T, hw), lambda h, i: (0, h)),
            pl.BlockSpec((4, HEAD_DIM), lambda h, i: (0, 0)),
            pl.BlockSpec((1, hw), lambda h, i: (0, 0)),
        ],
        out_specs=pl.BlockSpec((tile, hw), lambda h, i: (i, h)),
        out_shape=jax.ShapeDtypeStruct((T, D_MODEL), BF16),
        scratch_shapes=[pltpu.VMEM((2 * tile, LANES), F32), pltpu.VMEM((2 * tile, 2 * hw), F32),
                        pltpu.VMEM((2 * tile, tile), F32), pltpu.VMEM((2 * tile, tile), F32)],
        compiler_params=_cparams("arbitrary", "arbitrary"),
        name="diff_attn_prompt",
    )(qT, kTb, vb, lam, gsub)


def _topk_pick(g, posf, k, axis=1):
    picked = jnp.zeros(g.shape, jnp.bool_)
    idxs = []
    for _ in range(k):
        m = jnp.max(g, axis=axis, keepdims=True)
        idx = jnp.min(jnp.where(g == m, posf, float(4 * LANES)), axis=axis, keepdims=True)
        hit = posf == idx
        picked = jnp.logical_or(picked, hit)
        g = jnp.where(hit, KNOCKED, g)
        idxs.append(idx)
    return picked, idxs


def _moba_body(qT_ref, kT_ref, v_ref, km_ref, o_ref, ma, aa, mb, ab, sa_scr, sb_scr, *, tile):
    qi = pl.program_id(1)
    blk_shift = MOBA_BLOCK.bit_length() - 1
    bpt = tile // MOBA_BLOCK
    qT = qT_ref[...]
    qsT = qT * (HEAD_DIM ** -0.5 * LOG2E)
    lo = lax.broadcasted_iota(jnp.int32, (tile, 2 * HEAD_DIM), 1) < HEAD_DIM
    blk = lax.broadcasted_iota(jnp.int32, (HEAD_DIM, tile), 0)
    blkf = blk.astype(F32)
    own = qi * bpt + jnp.right_shift(lax.broadcasted_iota(jnp.int32, (HEAD_DIM, tile), 1), blk_shift)
    past = blk < own

    def select_bias(hh):
        g = jnp.dot(km_ref[hh], qT[hh * HEAD_DIM:(hh + 1) * HEAD_DIM, :], precision=HIGHEST,
                    preferred_element_type=F32)
        g = jnp.where(past, g, MASK_NEG)
        picked, _ = _topk_pick(g, blkf, MOBA_TOPK, axis=0)
        sel = jnp.logical_or(jnp.logical_and(picked, past), blk == own)
        return jnp.where(sel, 0.0, -SEL_BIAS)

    qa = jnp.concatenate([qsT[:HEAD_DIM, :], select_bias(0)], axis=0).T
    qb = jnp.concatenate([select_bias(1), qsT[HEAD_DIM:, :]], axis=0).T
    zq = jnp.zeros_like(qa)
    qc = jnp.concatenate([jnp.concatenate([qa, zq], axis=1), jnp.concatenate([zq, qb], axis=1)],
                         axis=0).astype(BF16)
    _init_flash_state(ma, aa)
    _init_flash_state(mb, ab)
    sub = lax.broadcasted_iota(jnp.int32, (HEAD_DIM, tile), 0)
    key_blk = jnp.right_shift(lax.broadcasted_iota(jnp.int32, (HEAD_DIM, tile), 1), blk_shift)
    vlane = lax.broadcasted_iota(jnp.int32, (tile, 2 * HEAD_DIM), 1)
    one = jnp.ones((tile, 2 * HEAD_DIM), BF16)

    def scores(kb, s_ref):
        off = pl.multiple_of(kb * tile, tile)
        kt = kT_ref[:, pl.ds(off, tile)]
        onehot = jnp.where(sub == kb * bpt + key_blk, 1.0, 0.0).astype(BF16)
        kc = jnp.concatenate([kt[:HEAD_DIM, :], onehot, onehot, kt[HEAD_DIM:, :]], axis=0)
        s_ref[...] = jnp.dot(qc, kc, preferred_element_type=F32)

    def consume(kb, s_ref, masked):
        off = pl.multiple_of(kb * tile, tile)
        vv = v_ref[pl.ds(off, tile), :]
        sa = s_ref[:tile, :]
        sb = s_ref[tile:, :]
        if masked:
            row = lax.broadcasted_iota(jnp.int32, (tile, tile), 0)
            col = lax.broadcasted_iota(jnp.int32, (tile, tile), 1)
            keep = col <= row
            sa = jnp.where(keep, sa, MASK_NEG)
            sb = jnp.where(keep, sb, MASK_NEG)
        _flash_step(sa, jnp.where(vlane < HEAD_DIM, vv, one), ma, aa)
        _flash_step(sb, jnp.where(vlane >= HEAD_DIM, vv, one), mb, ab)

    _run_pipelined(qi, scores, consume, sa_scr, sb_scr)
    acc_a = aa[...]
    acc_b = ab[...]
    oa = acc_a / pltpu.roll(acc_a, HEAD_DIM, 1)
    ob = acc_b / pltpu.roll(acc_b, HEAD_DIM, 1)
    o_ref[...] = jnp.where(lo, oa, ob).astype(o_ref.dtype)


def _moba_prompt(qT, kTb, vb, km_aug, *, tile):
    T = qT.shape[1]
    hw = 2 * HEAD_DIM
    assert T % tile == 0 and tile % MOBA_BLOCK == 0 and T // MOBA_BLOCK <= HEAD_DIM
    return pl.pallas_call(
        functools.partial(_moba_body, tile=tile),
        grid=(N_HEADS_B // 2, T // tile),
        in_specs=[
            pl.BlockSpec((hw, tile), lambda h, i: (h, i)),
            pl.BlockSpec((hw, T), lambda h, i: (h, 0)),
            pl.BlockSpec((T, hw), lambda h, i: (0, h)),
            pl.BlockSpec((2, HEAD_DIM, HEAD_DIM), lambda h, i: (h, 0, 0)),
        ],
        out_specs=pl.BlockSpec((tile, hw), lambda h, i: (i, h)),
        out_shape=jax.ShapeDtypeStruct((T, D_MODEL), BF16),
        scratch_shapes=[
            pltpu.VMEM((tile, LANES), F32), pltpu.VMEM((tile, hw), F32),
            pltpu.VMEM((tile, LANES), F32), pltpu.VMEM((tile, hw), F32),
            pltpu.VMEM((2 * tile, tile), F32), pltpu.VMEM((2 * tile, tile), F32),
        ],
        compiler_params=_cparams("arbitrary", "arbitrary"),
        name="moba_attn_prompt",
    )(qT, kTb, vb, km_aug)


def _moba_kmean_per_head(kmean):
    n_blk = kmean.shape[0]
    km = kmean.reshape(n_blk, N_HEADS_B, HEAD_DIM).transpose(1, 0, 2)
    return jnp.pad(km, ((0, 0), (0, HEAD_DIM - n_blk), (0, 0)))


def _oproj_body(x_ref, o_ref, w_ref, y_ref):
    y_ref[...] = x_ref[...] + jnp.dot(o_ref[...], w_ref[...], preferred_element_type=F32)


def _oproj(x, o, w, *, tile):
    T = x.shape[0]
    rowblk = pl.BlockSpec((tile, D_MODEL), lambda i: (i, 0))
    return pl.pallas_call(
        _oproj_body,
        grid=(T // tile,),
        in_specs=[rowblk, rowblk, pl.BlockSpec((D_MODEL, D_MODEL), lambda i: (0, 0))],
        out_specs=rowblk,
        out_shape=jax.ShapeDtypeStruct((T, D_MODEL), F32),
        compiler_params=_cparams("arbitrary"),
        name="attn_out_proj",
    )(x, o, w)


def _swiglu_chunk(hn, wg, wu, wd):
    a = jnp.dot(hn, wg, preferred_element_type=F32)
    u = jnp.dot(hn, wu, preferred_element_type=F32)
    act = (a * _sigmoid(a) * u).astype(BF16)
    return jnp.dot(act, wd, preferred_element_type=F32)


def _ffn_body(x_ref, g_ref, wg_ref, wu_ref, wd_ref, y_ref, hn_scr, acc_scr):
    f = pl.program_id(1)

    @pl.when(f == 0)
    def _():
        x = x_ref[...]
        hn_scr[...] = _rms_rows(x, g_ref[...]).astype(BF16)
        acc_scr[...] = x

    acc_scr[...] += _swiglu_chunk(hn_scr[...], wg_ref[...], wu_ref[...], wd_ref[...])

    @pl.when(f == pl.num_programs(1) - 1)
    def _():
        y_ref[...] = acc_scr[...]


def _ffn(x, g, wg, wu, wd, *, tile, tf):
    T = x.shape[0]
    F = wg.shape[1]
    assert F % tf == 0
    rowblk = pl.BlockSpec((tile, D_MODEL), lambda i, f: (i, 0))
    return pl.pallas_call(
        _ffn_body,
        grid=(T // tile, F // tf),
        in_specs=[
            rowblk,
            pl.BlockSpec((1, D_MODEL), lambda i, f: (0, 0)),
            pl.BlockSpec((D_MODEL, tf), lambda i, f: (0, f)),
            pl.BlockSpec((D_MODEL, tf), lambda i, f: (0, f)),
            pl.BlockSpec((tf, D_MODEL), lambda i, f: (f, 0)),
        ],
        out_specs=rowblk,
        out_shape=jax.ShapeDtypeStruct((T, D_MODEL), F32),
        scratch_shapes=[pltpu.VMEM((tile, D_MODEL), BF16), pltpu.VMEM((tile, D_MODEL), F32)],
        compiler_params=_cparams("arbitrary", "arbitrary"),
        name="dense_swiglu",
    )(x, g, wg, wu, wd)


ROUTE_MASK = 8
ROUTE_RANK = 16


def _router_body(x_ref, g_ref, wr_ref, hn_ref, route_ref, routeT_ref, count_ref):
    hf = _rms_rows(x_ref[...], g_ref[...])
    hn_ref[...] = hf.astype(BF16)
    logits = jnp.dot(hf, wr_ref[...], precision=HIGHEST, preferred_element_type=F32)
    lane = lax.broadcasted_iota(jnp.int32, logits.shape, 1)
    lanef = lane.astype(F32)
    logits = jnp.where(lane < N_EXPERTS, logits, MASK_NEG)
    v1 = jnp.max(logits, axis=1, keepdims=True)
    i1 = jnp.min(jnp.where(logits == v1, lanef, float(LANES)), axis=1, keepdims=True)
    rest = jnp.where(lanef == i1, KNOCKED, logits)
    v2 = jnp.max(rest, axis=1, keepdims=True)
    i2 = jnp.min(jnp.where(rest == v2, lanef, float(LANES)), axis=1, keepdims=True)
    e2 = jnp.exp(v2 - v1)
    g1 = 1.0 / (1.0 + e2)
    g2 = e2 / (1.0 + e2)
    weights = jnp.where(lanef == i1, g1, 0.0) + jnp.where(lanef == i2, g2, 0.0)
    le = lanef - float(ROUTE_RANK)
    routed = jnp.where(jnp.logical_or(le == i1, le == i2), 1.0, 0.0)
    t = hf.shape[0]
    before = (lax.broadcasted_iota(jnp.int32, (t, t), 1) < lax.broadcasted_iota(jnp.int32, (t, t), 0))
    rank = jnp.dot(jnp.where(before, 1.0, 0.0).astype(BF16), routed.astype(BF16),
                   preferred_element_type=F32)
    route = weights + pltpu.roll(routed, LANES - (ROUTE_RANK - ROUTE_MASK), 1) + rank * routed
    route_ref[...] = route
    eye = jnp.where(lax.broadcasted_iota(jnp.int32, (LANES, LANES), 0)
                    == lax.broadcasted_iota(jnp.int32, (LANES, LANES), 1), 1.0, 0.0)
    routeT_ref[...] = lax.dot_general(eye, route, NT_DIMS, precision=HIGHEST,
                                      preferred_element_type=F32)
    count_ref[0] = jnp.sum(routed, axis=0, keepdims=True).astype(jnp.int32)


def _router(x, g, wr, *, tile):
    T = x.shape[0]
    n = T // tile
    rowblk = pl.BlockSpec((tile, D_MODEL), lambda i: (i, 0))
    return pl.pallas_call(
        _router_body,
        grid=(n,),
        in_specs=[rowblk, pl.BlockSpec((1, D_MODEL), lambda i: (0, 0)),
                  pl.BlockSpec((D_MODEL, LANES), lambda i: (0, 0))],
        out_specs=[rowblk, pl.BlockSpec((tile, LANES), lambda i: (i, 0)),
                   pl.BlockSpec((LANES, tile), lambda i: (0, i)),
                   pl.BlockSpec((1, 1, LANES), lambda i: (i, 0, 0))],
        out_shape=[jax.ShapeDtypeStruct((T, D_MODEL), BF16), jax.ShapeDtypeStruct((T, LANES), F32),
                   jax.ShapeDtypeStruct((LANES, T), F32), jax.ShapeDtypeStruct((n, 1, LANES), jnp.int32)],
        compiler_params=_cparams("arbitrary"),
        name="moe_router",
    )(x, g, wr)


def _experts_body(cnt_ref, hn_ref, route_ref, routeT_ref, wg_ref, wu_ref, wd_ref, y_ref,
                  x_scr, y_scr, *, cap):
    i = pl.program_id(0)
    e = pl.program_id(1)
    f = pl.program_id(2)
    last_f = f == pl.num_programs(2) - 1
    t = hn_ref.shape[0]
    n_chunks = lax.div(cnt_ref[i, e] + (cap - 1), cap)

    @pl.when(jnp.logical_and(e == 0, f == 0))
    def _():
        y_ref[...] = jnp.zeros(y_ref.shape, F32)

    @pl.when(f == 0)
    def _():
        rankT = routeT_ref[pl.ds(ROUTE_RANK + e, 1), :]
        maskT = routeT_ref[pl.ds(ROUTE_MASK + e, 1), :]
        rankT = jnp.where(maskT > 0.5, rankT, -1.0)
        slot = lax.broadcasted_iota(jnp.int32, (cap, t), 0).astype(F32)

        def gather(c, carry):
            r0 = pl.multiple_of(c * cap, 16)
            onehot = jnp.where(rankT - (c * cap).astype(F32) == slot, 1.0, 0.0).astype(BF16)
            x_scr[pl.ds(r0, cap), :] = jnp.dot(onehot, hn_ref[...],
                                               preferred_element_type=F32).astype(BF16)
            y_scr[pl.ds(r0, cap), :] = jnp.zeros((cap, D_MODEL), F32)
            return carry

        lax.fori_loop(0, n_chunks, gather, 0)

    def expert(c, carry):
        r0 = pl.multiple_of(c * cap, 16)
        y_scr[pl.ds(r0, cap), :] += _swiglu_chunk(x_scr[pl.ds(r0, cap), :], wg_ref[...], wu_ref[...],
                                                  wd_ref[...])
        return carry

    lax.fori_loop(0, n_chunks, expert, 0)

    @pl.when(last_f)
    def _():
        route = route_ref[...]
        lane = lax.broadcasted_iota(jnp.int32, route.shape, 1)
        pick = lambda off: jnp.sum(jnp.where(lane == off + e, route, 0.0), axis=1, keepdims=True)
        w_e = pick(0)
        rank = jnp.where(pick(ROUTE_MASK) > 0.5, pick(ROUTE_RANK), -1.0)
        slot = lax.broadcasted_iota(jnp.int32, (t, 2 * cap), 1)
        slot = jnp.where(slot < cap, slot, slot - cap).astype(F32)

        def scatter(c, carry):
            r0 = pl.multiple_of(c * cap, 16)
            onehot = jnp.where(rank - (c * cap).astype(F32) == slot, 1.0, 0.0).astype(BF16)
            y = y_scr[pl.ds(r0, cap), :]
            y_hi = y.astype(BF16)
            y_lo = (y - y_hi.astype(F32)).astype(BF16)
            back = jnp.dot(onehot, jnp.concatenate([y_hi, y_lo], axis=0),
                           preferred_element_type=F32)
            y_ref[...] += w_e * back
            return carry

        lax.fori_loop(0, n_chunks, scatter, 0)


def _experts(counts, hn, route, routeT, wg, wu, wd, *, tile, tf, cap):
    T = hn.shape[0]
    E, _, F = wg.shape
    assert F % tf == 0 and cap % 16 == 0
    rows = -(-tile // cap) * cap
    rowblk = pl.BlockSpec((tile, D_MODEL), lambda i, e, f, cnt: (i, 0))
    grid_spec = pltpu.PrefetchScalarGridSpec(
        num_scalar_prefetch=1,
        grid=(T // tile, E, F // tf),
        in_specs=[
            rowblk,
            pl.BlockSpec((tile, LANES), lambda i, e, f, cnt: (i, 0)),
            pl.BlockSpec((LANES, tile), lambda i, e, f, cnt: (0, i)),
            pl.BlockSpec((None, D_MODEL, tf), lambda i, e, f, cnt: (e, 0, f)),
            pl.BlockSpec((None, D_MODEL, tf), lambda i, e, f, cnt: (e, 0, f)),
            pl.BlockSpec((None, tf, D_MODEL), lambda i, e, f, cnt: (e, f, 0)),
        ],
        out_specs=rowblk,
        scratch_shapes=[pltpu.VMEM((rows, D_MODEL), BF16), pltpu.VMEM((rows, D_MODEL), F32)],
    )
    return pl.pallas_call(
        functools.partial(_experts_body, cap=cap),
        grid_spec=grid_spec,
        out_shape=jax.ShapeDtypeStruct((T, D_MODEL), F32),
        compiler_params=_cparams("arbitrary", "arbitrary", "arbitrary"),
        name="moe_experts",
    )(counts, hn, route, routeT, wg, wu, wd)


def _ple_body(x_ref, *rest, with_addend):
    if with_addend:
        add_ref, g_ref, wgate_ref, p_ref, win_ref, y_ref = rest
        x = x_ref[...] + add_ref[...]
    else:
        g_ref, wgate_ref, p_ref, win_ref, y_ref = rest
        x = x_ref[...]
    hn = _rms_rows(x, g_ref[...]).astype(BF16)
    gate = _sigmoid(jnp.dot(hn, wgate_ref[...], preferred_element_type=F32))
    emb = jnp.dot(p_ref[...].astype(BF16), win_ref[...], preferred_element_type=F32)
    y_ref[...] = x + gate * emb


def _ple(x, addend, g, wgate, p, win, *, tile):
    T = x.shape[0]
    pd = p.shape[1]
    rowblk = pl.BlockSpec((tile, D_MODEL), lambda i: (i, 0))
    acts = (x,) if addend is None else (x, addend)
    return pl.pallas_call(
        functools.partial(_ple_body, with_addend=addend is not None),
        grid=(T // tile,),
        in_specs=[rowblk] * len(acts) + [
            pl.BlockSpec((1, D_MODEL), lambda i: (0, 0)),
            pl.BlockSpec((D_MODEL, D_MODEL), lambda i: (0, 0)),
            pl.BlockSpec((tile, pd), lambda i: (i, 0)),
            pl.BlockSpec((pd, D_MODEL), lambda i: (0, 0)),
        ],
        out_specs=rowblk,
        out_shape=jax.ShapeDtypeStruct((T, D_MODEL), F32),
        compiler_params=_cparams("arbitrary"),
        name="per_layer_embedding",
    )(*acts, g, wgate, p, win)


PAGES_PER_STEP = 8


def _diff_decode_body(pt_ref, qbd_ref, knew_ref, vnew_ref, lam_ref, gsub_ref, *rest,
                      dec_seq, lam_init):
    kT_refs = rest[:PAGES_PER_STEP]
    v_refs = rest[PAGES_PER_STEP:2 * PAGES_PER_STEP]
    o_ref, m_ref, l_ref, acc_ref = rest[2 * PAGES_PER_STEP:]
    j = pl.program_id(1)
    n_rows = qbd_ref.shape[0]

    @pl.when(j == 0)
    def _():
        _init_softmax_state(m_ref, l_ref, acc_ref)

    qbd = qbd_ref[...]
    s = jnp.concatenate(
        [jnp.dot(qbd, r[...].astype(BF16), preferred_element_type=F32) for r in kT_refs], axis=1)
    m_old = m_ref[...]
    m_new = jnp.maximum(m_old, jnp.max(s, axis=1, keepdims=True))
    p = jnp.exp(s - m_new)
    alpha = jnp.exp(m_old - m_new)
    l_ref[...] = alpha * l_ref[...] + jnp.sum(p, axis=1, keepdims=True)
    acc = alpha * acc_ref[...]
    for i, r in enumerate(v_refs):
        vcat = jnp.concatenate(
            [r[pl.ds(h, PAGE_SIZE, stride=N_HEADS_A), :] for h in range(N_HEADS_A)], axis=1)
        acc += jnp.dot(p[:, i * PAGE_SIZE:(i + 1) * PAGE_SIZE].astype(BF16), vcat.astype(BF16),
                       preferred_element_type=F32)
    acc_ref[...] = acc
    m_ref[...] = m_new

    @pl.when(j == pl.num_programs(1) - 1)
    def _():
        sn = lax.dot_general(qbd, knew_ref[...], NT_DIMS, preferred_element_type=F32)
        row = lax.broadcasted_iota(jnp.int32, sn.shape, 0)
        col = lax.broadcasted_iota(jnp.int32, sn.shape, 1)
        half = n_rows // 2
        t_of_row = jnp.right_shift(jnp.bitwise_and(row, half - 1), 3)
        sn = jnp.where(jnp.logical_and(col <= t_of_row, col < dec_seq), sn, MASK_NEG)
        _online_update(sn, vnew_ref[...], m_ref, l_ref, acc_ref)
        o = acc_ref[...] / l_ref[...]
        lam = _diff_lambda(lam_ref, lam_init)
        od = o[:half, :] - lam * o[half:, :]
        r2 = lax.broadcasted_iota(jnp.int32, od.shape, 0)
        c2 = lax.broadcasted_iota(jnp.int32, od.shape, 1)
        own = jnp.right_shift(c2, 7) == jnp.bitwise_and(r2, N_HEADS_A - 1)
        od = jnp.where(own, od, 0.0)
        ms = jnp.sum(od * od, axis=1, keepdims=True) / (2 * HEAD_DIM)
        od = od * lax.rsqrt(ms + EPS) * gsub_ref[...] * (1.0 - lam_init)
        rows = [jnp.sum(od[t * N_HEADS_A:(t + 1) * N_HEADS_A, :], axis=0, keepdims=True)
                for t in range(dec_seq)]
        o_ref[...] = jnp.concatenate(rows, axis=0)


def _diff_decode(page_table, qbd, knew, vnew, lam, gsub_tiled, kT_pages, v_rows, *, dec_seq,
                 lam_init):
    nb, n_pages = page_table.shape
    assert n_pages % PAGES_PER_STEP == 0
    n_rows = qbd.shape[1]
    kv_rows = PAGE_SIZE * N_HEADS_A

    def k_spec(i):
        return pl.BlockSpec((None, D_MODEL, PAGE_SIZE),
                            lambda b, j, pt: (pt[b, j * PAGES_PER_STEP + i], 0, 0))

    def v_spec(i):
        return pl.BlockSpec((kv_rows, 2 * HEAD_DIM),
                            lambda b, j, pt: (pt[b, j * PAGES_PER_STEP + i], 0))

    per_seq = lambda b, j, pt: (b, 0, 0)
    const2 = lambda b, j, pt: (0, 0)
    grid_spec = pltpu.PrefetchScalarGridSpec(
        num_scalar_prefetch=1,
        grid=(nb, n_pages // PAGES_PER_STEP),
        in_specs=[
            pl.BlockSpec((None, n_rows, D_MODEL), per_seq),
            pl.BlockSpec((None, PAGE_SIZE, D_MODEL), per_seq),
            pl.BlockSpec((None, PAGE_SIZE, D_MODEL), per_seq),
            pl.BlockSpec((4, HEAD_DIM), const2),
            pl.BlockSpec((1, D_MODEL), const2),
        ] + [k_spec(i) for i in range(PAGES_PER_STEP)] + [v_spec(i) for i in range(PAGES_PER_STEP)],
        out_specs=pl.BlockSpec((None, dec_seq, D_MODEL), per_seq),
        scratch_shapes=[pltpu.VMEM((n_rows, 1), F32), pltpu.VMEM((n_rows, 1), F32),
                        pltpu.VMEM((n_rows, D_MODEL), F32)],
    )
    return pl.pallas_call(
        functools.partial(_diff_decode_body, dec_seq=dec_seq, lam_init=lam_init),
        grid_spec=grid_spec,
        out_shape=jax.ShapeDtypeStruct((nb, dec_seq, D_MODEL), F32),
        compiler_params=_cparams("arbitrary", "arbitrary"),
        name="diff_attn_decode",
    )(page_table, qbd, knew, vnew, lam, gsub_tiled,
      *([kT_pages] * PAGES_PER_STEP), *([v_rows] * PAGES_PER_STEP))


def _moba_gate_body(pt_ref, qbd_ref, *rest):
    kT_refs = rest[:PAGES_PER_STEP]
    idx_ref, km_scr = rest[PAGES_PER_STEP:]
    j = pl.program_id(1)
    pages_per_block = MOBA_BLOCK // PAGE_SIZE
    blocks_per_step = PAGES_PER_STEP // pages_per_block

    @pl.when(j == 0)
    def _():
        km_scr[...] = jnp.zeros(km_scr.shape, F32)

    lane = lax.broadcasted_iota(jnp.int32, km_scr.shape, 1)
    km = km_scr[...]
    for bb in range(blocks_per_step):
        pages = kT_refs[bb * pages_per_block][...]
        for i in range(1, pages_per_block):
            pages = pages + kT_refs[bb * pages_per_block + i][...]
        tot = jnp.sum(pages, axis=1, keepdims=True)
        km = jnp.where(lane == j * blocks_per_step + bb, tot * (1.0 / MOBA_BLOCK), km)
    km_scr[...] = km

    @pl.when(j == pl.num_programs(1) - 1)
    def _():
        n_blk = pl.num_programs(1) * blocks_per_step
        g = jnp.dot(qbd_ref[...], km_scr[...], precision=HIGHEST, preferred_element_type=F32)
        glane = lax.broadcasted_iota(jnp.int32, g.shape, 1)
        glanef = glane.astype(F32)
        g = jnp.where(glane < n_blk, g, MASK_NEG)
        _, idxs = _topk_pick(g, glanef, MOBA_TOPK)
        out = jnp.zeros(g.shape, F32)
        for r, idx in enumerate(idxs):
            out = jnp.where(glane == r, idx, out)
        idx_ref[...] = out.astype(jnp.int32)


def _moba_gate(page_table, qbd, kT_pages):
    nb, n_pages = page_table.shape
    assert n_pages % PAGES_PER_STEP == 0 and n_pages * PAGE_SIZE // MOBA_BLOCK <= LANES
    n_rows = qbd.shape[1]

    def k_spec(i):
        return pl.BlockSpec((None, D_MODEL, PAGE_SIZE),
                            lambda b, j, pt: (pt[b, j * PAGES_PER_STEP + i], 0, 0))

    per_seq = lambda b, j, pt: (b, 0, 0)
    grid_spec = pltpu.PrefetchScalarGridSpec(
        num_scalar_prefetch=1,
        grid=(nb, n_pages // PAGES_PER_STEP),
        in_specs=[pl.BlockSpec((None, n_rows, D_MODEL), per_seq)]
        + [k_spec(i) for i in range(PAGES_PER_STEP)],
        out_specs=pl.BlockSpec((None, n_rows, LANES), per_seq),
        scratch_shapes=[pltpu.VMEM((D_MODEL, LANES), F32)],
    )
    return pl.pallas_call(
        _moba_gate_body,
        grid_spec=grid_spec,
        out_shape=jax.ShapeDtypeStruct((nb, n_rows, LANES), jnp.int32),
        compiler_params=_cparams("arbitrary", "arbitrary"),
        name="moba_decode_gate",
    )(page_table, qbd, *([kT_pages] * PAGES_PER_STEP))


def _moba_sel_body(phys_ref, q_ref, knew_ref, vnew_ref, *rest, dec_seq, n_sel):
    k_refs = rest[:dec_seq * n_sel]
    v_refs = rest[dec_seq * n_sel:2 * dec_seq * n_sel]
    (o_ref,) = rest[2 * dec_seq * n_sel:]
    q = q_ref[...]
    lane = lax.broadcasted_iota(jnp.int32, (1, PAGE_SIZE), 1)
    olane = lax.broadcasted_iota(jnp.int32, o_ref.shape, 1)
    out = jnp.zeros(o_ref.shape, F32)
    for t in range(dec_seq):
        qcol = q[:, t:t + 1]
        ks = [k_refs[t * n_sel + s][...] for s in range(n_sel)] + [knew_ref[...]]
        vs = [v_refs[t * n_sel + s][...] for s in range(n_sel)] + [vnew_ref[...]]
        scores = [jnp.sum(k * qcol, axis=0, keepdims=True) for k in ks]
        scores[-1] = jnp.where(lane <= t, scores[-1], MASK_NEG)
        m = scores[0].max(axis=1, keepdims=True)
        for sc in scores[1:]:
            m = jnp.maximum(m, sc.max(axis=1, keepdims=True))
        ps = [jnp.exp(sc - m) for sc in scores]
        l = ps[0].sum(axis=1, keepdims=True)
        for pp in ps[1:]:
            l = l + pp.sum(axis=1, keepdims=True)
        o = jnp.sum(vs[0] * ps[0], axis=1, keepdims=True)
        for vv, pp in zip(vs[1:], ps[1:]):
            o = o + jnp.sum(vv * pp, axis=1, keepdims=True)
        out = jnp.where(olane == t, o / l, out)
    o_ref[...] = out


def _moba_decode_attend(phys, qT, knewT, vnewT, kT_slices, vT_slices, *, dec_seq):
    nb, nh, n_slices = phys.shape
    n_sel = n_slices // dec_seq

    def sel_spec(s):
        return pl.BlockSpec((None, None, HEAD_DIM, PAGE_SIZE),
                            lambda b, h, ph: (ph[b, h, s], h, 0, 0))

    per_head = pl.BlockSpec((None, None, HEAD_DIM, PAGE_SIZE), lambda b, h, ph: (b, h, 0, 0))
    grid_spec = pltpu.PrefetchScalarGridSpec(
        num_scalar_prefetch=1,
        grid=(nb, nh),
        in_specs=[per_head, per_head, per_head]
        + [sel_spec(s) for s in range(n_slices)] + [sel_spec(s) for s in range(n_slices)],
        out_specs=per_head,
    )
    return pl.pallas_call(
        functools.partial(_moba_sel_body, dec_seq=dec_seq, n_sel=n_sel),
        grid_spec=grid_spec,
        out_shape=jax.ShapeDtypeStruct((nb, nh, HEAD_DIM, PAGE_SIZE), F32),
        compiler_params=_cparams("arbitrary", "arbitrary"),
        name="moba_decode_attend",
    )(phys, qT, knewT, vnewT, *([kT_slices] * n_slices), *([vT_slices] * n_slices))


def _rope_tables(pos):
    inv = ROPE_THETA ** (-jnp.arange(0, HEAD_DIM, 2, dtype=F32) / HEAD_DIM)
    ang = pos.astype(F32)[:, None] * inv[None, :]
    ang = jnp.concatenate([ang, ang], axis=-1)
    return jnp.cos(ang).T, jnp.sin(ang).T


def _col(v):
    return v.reshape(-1, 1)


def _row(v):
    return v.reshape(1, -1)


def _pad_lanes(a, n):
    return jnp.pad(a, [(0, 0)] * (a.ndim - 1) + [(0, n - a.shape[-1])])


def kernel(x_prompt, x_sample, cache_k_a, cache_v_a, cache_k_b, cache_v_b, page_table, p_prompt, p_sample,
           g_mix, g_ffn, g_ple, w_ple_gate, w_ple_in, w_qkv_a, g_q_a, g_k_a, lam_a, g_sub_a, w_o_a,
           w_qkv_b, g_q_b, g_k_b, w_o_b, w_ff_gate, w_ff_up, w_ff_down, w_router, w_ex_gate, w_ex_up, w_ex_down):
    _, T, D = x_prompt.shape
    nb, dec_seq, _ = x_sample.shape
    n_dec = nb * dec_seq
    n_pool = cache_k_a.shape[1]
    past_len = page_table.shape[1] * PAGE_SIZE
    assert D == D_MODEL and n_dec % 8 == 0 and dec_seq <= PAGE_SIZE
    pt = page_table.astype(jnp.int32)

    tile_p = 512 if T % 512 == 0 else MOBA_BLOCK
    tile_a = 1024 if T % 1024 == 0 else tile_p
    xp = x_prompt.reshape(T, D)
    xs = x_sample.reshape(n_dec, D)
    cos_p, sin_p = _rope_tables(jnp.arange(T))
    pos_s = past_len + jnp.tile(jnp.arange(dec_seq), nb)
    cos_s, sin_s = _rope_tables(pos_s)

    lam_init = 0.8 - 0.6 * math.exp(-0.3 * 0)
    wqkT = w_qkv_a[0][:, :2 * D].T.astype(BF16)
    wv = w_qkv_a[0][:, 2 * D:].astype(BF16)
    wo = w_o_a[0].astype(BF16)
    qkv_args = (_row(g_mix[0]), wqkT, wv, _col(g_q_a[0]), _col(g_k_a[0]))
    qT_p, kT_p, kTb_p, v_p, vb_p = _qkv(xp, *qkv_args, cos_p, sin_p, tile=MOBA_BLOCK, with_kmean=False)
    qT_s, kT_s, _, v_s, _ = _qkv(xs, *qkv_args, cos_s, sin_s, tile=n_dec, with_kmean=False)
    ka_p = kT_p.reshape(1, 1, N_HEADS_A, 2, HEAD_DIM, T).transpose(0, 1, 5, 2, 3, 4)
    va_p = v_p.reshape(1, 1, T, N_HEADS_A, 2 * HEAD_DIM)
    ka_s = kT_s.T.reshape(1, nb, dec_seq, N_HEADS_A, 2, HEAD_DIM)
    va_s = v_s.reshape(1, nb, dec_seq, N_HEADS_A, 2 * HEAD_DIM)

    o_p = _diff_attn_prompt(qT_p, kTb_p, vb_p, lam_a[0], _row(g_sub_a[0]), lam_init=lam_init,
                            tile=tile_a)

    q5 = (qT_s.T * (HEAD_DIM ** -0.5)).reshape(nb, dec_seq, N_HEADS_A, 2, HEAD_DIM)
    eye_h = jnp.eye(N_HEADS_A, dtype=F32)
    eye_c = jnp.eye(2, dtype=F32)
    qbd = jnp.einsum('bthcd,hH,cC->bcthHCd', q5, eye_h, eye_c).reshape(nb, 2 * dec_seq * N_HEADS_A, D)
    qbd = qbd.astype(BF16)
    knew = jnp.pad(kT_s.T.reshape(nb, dec_seq, D), ((0, 0), (0, PAGE_SIZE - dec_seq), (0, 0))).astype(BF16)
    vnew = jnp.pad(v_s.reshape(nb, dec_seq, D), ((0, 0), (0, PAGE_SIZE - dec_seq), (0, 0))).astype(BF16)
    kT_pages_a = cache_k_a[0].transpose(0, 2, 3, 4, 1).reshape(n_pool, D, PAGE_SIZE)
    v_rows_a = cache_v_a[0].reshape(n_pool * PAGE_SIZE * N_HEADS_A, 2 * HEAD_DIM)
    o_s = _diff_decode(pt, qbd, knew, vnew, lam_a[0], _row(jnp.tile(g_sub_a[0], N_HEADS_A)),
                       kT_pages_a, v_rows_a, dec_seq=dec_seq, lam_init=lam_init)
    o_s = o_s.reshape(n_dec, D).astype(BF16)

    xp = _oproj(xp, o_p, wo, tile=tile_p)
    xs = _oproj(xs, o_s, wo, tile=n_dec)

    wg = w_ff_gate[0].astype(BF16)
    wu = w_ff_up[0].astype(BF16)
    wd = w_ff_down[0].astype(BF16)
    tf = wg.shape[1] // 2
    xp = _ffn(xp, _row(g_ffn[0]), wg, wu, wd, tile=tile_p, tf=tf)
    xs = _ffn(xs, _row(g_ffn[0]), wg, wu, wd, tile=n_dec, tf=tf)

    wpg = w_ple_gate[0].astype(BF16)
    wpi = w_ple_in[0].astype(BF16)
    xp = _ple(xp, None, _row(g_ple[0]), wpg, p_prompt[0].reshape(T, -1), wpi, tile=tile_p)
    xs = _ple(xs, None, _row(g_ple[0]), wpg, p_sample[0].reshape(n_dec, -1), wpi, tile=n_dec)

    wqkT = w_qkv_b[0][:, :2 * D].T.astype(BF16)
    wv = w_qkv_b[0][:, 2 * D:].astype(BF16)
    wo = w_o_b[0].astype(BF16)
    qkv_args = (_row(g_mix[1]), wqkT, wv, _col(g_q_b[0]), _col(g_k_b[0]))
    qT_p, kT_p, kTb_p, v_p, vb_p, kmean = _qkv(xp, *qkv_args, cos_p, sin_p, tile=MOBA_BLOCK, with_kmean=True)
    qT_s, kT_s, _, v_s, _ = _qkv(xs, *qkv_args, cos_s, sin_s, tile=n_dec, with_kmean=False)
    kb_p = kT_p.reshape(1, 1, N_HEADS_B, HEAD_DIM, T).transpose(0, 1, 4, 2, 3)
    vb_out_p = v_p.reshape(1, 1, T, N_HEADS_B, HEAD_DIM)
    kb_s = kT_s.T.reshape(1, nb, dec_seq, N_HEADS_B, HEAD_DIM)
    vb_out_s = v_s.reshape(1, nb, dec_seq, N_HEADS_B, HEAD_DIM)

    o_p = _moba_prompt(qT_p, kTb_p, vb_p, _moba_kmean_per_head(kmean), tile=tile_a)

    q4 = qT_s.T.reshape(nb, dec_seq, N_HEADS_B, HEAD_DIM)
    qbd2 = jnp.einsum('bthd,hH->bhtHd', q4, jnp.eye(N_HEADS_B, dtype=F32)).reshape(nb, N_HEADS_B * dec_seq, D)
    kT_pages_b = cache_k_b[0].transpose(0, 2, 3, 1).reshape(n_pool, D, PAGE_SIZE)
    idx = _moba_gate(pt, qbd2, kT_pages_b)[:, :, :MOBA_TOPK]
    idx = idx.reshape(nb, N_HEADS_B, dec_seq, MOBA_TOPK)
    ppb = MOBA_BLOCK // PAGE_SIZE
    logical = (idx[..., None] * ppb + jnp.arange(ppb)).reshape(nb, N_HEADS_B, -1)
    per_head = lambda aT: aT.reshape(N_HEADS_B, HEAD_DIM, nb, dec_seq).transpose(2, 0, 1, 3)
    qT_h = _pad_lanes(per_head(qT_s) * (HEAD_DIM ** -0.5), PAGE_SIZE)
    knewT = _pad_lanes(per_head(kT_s), PAGE_SIZE)
    vnewT = _pad_lanes(per_head(v_s.T), PAGE_SIZE)
    kT_slices_b = cache_k_b[0].transpose(0, 2, 3, 1)
    vT_slices_b = cache_v_b[0].transpose(0, 2, 3, 1)
    hit = logical[..., None] == jnp.arange(pt.shape[1], dtype=jnp.int32)
    phys = jnp.sum(jnp.where(hit, pt[:, None, None, :], 0), axis=-1)
    oT = _moba_decode_attend(phys, qT_h, knewT, vnewT, kT_slices_b, vT_slices_b, dec_seq=dec_seq)
    o_s = oT[..., :dec_seq].transpose(0, 3, 1, 2).reshape(n_dec, D).astype(BF16)

    xp = _oproj(xp, o_p, wo, tile=tile_p)
    xs = _oproj(xs, o_s, wo, tile=n_dec)

    wr = _pad_lanes(w_router[0], LANES)
    weg = w_ex_gate[0].astype(BF16)
    weu = w_ex_up[0].astype(BF16)
    wed = w_ex_down[0].astype(BF16)
    tfe = weg.shape[2] // 4
    def moe(x, tile, cap):
        hn, route, routeT, cnt = _router(x, _row(g_ffn[1]), wr, tile=tile)
        counts = cnt[:, 0, ROUTE_RANK:ROUTE_RANK + N_EXPERTS]
        return _experts(counts, hn, route, routeT, weg, weu, wed, tile=tile, tf=tfe, cap=cap)

    tile_e = 1024 if T % 1024 == 0 else tile_p
    moe_p = moe(xp, tile_e, (5 * tile_e // 16 + 15) // 16 * 16)
    moe_s = moe(xs, n_dec, (n_dec // 2 + 15) // 16 * 16)

    wpg = w_ple_gate[1].astype(BF16)
    wpi = w_ple_in[1].astype(BF16)
    xp = _ple(xp, moe_p, _row(g_ple[1]), wpg, p_prompt[1].reshape(T, -1), wpi, tile=tile_p)
    xs = _ple(xs, moe_s, _row(g_ple[1]), wpg, p_sample[1].reshape(n_dec, -1), wpi, tile=n_dec)

    return (xp.reshape(1, T, D), xs.reshape(nb, dec_seq, D), ka_p, va_p, kb_p, vb_out_p,
            ka_s, va_s, kb_s, vb_out_s)
```

```python
import functools
import math

import jax
import jax.numpy as jnp
from jax import lax
from jax.experimental import pallas as pl
from jax.experimental.pallas import tpu as pltpu

F32 = jnp.float32
BF16 = jnp.bfloat16
HIGHEST = lax.Precision.HIGHEST

D_MODEL = 1024
HEAD_DIM = 64
N_HEADS_A = 8
N_HEADS_B = 16
PAGE_SIZE = 128
MOBA_BLOCK = 256
MOBA_TOPK = 3
ROPE_THETA = 10000.0
N_EXPERTS = 8
EPS = 1e-6
LANES = 128
MASK_NEG = -1e30
KNOCKED = -3e38
SEL_BIAS = float(2 ** 100)
VMEM_LIMIT = 56 * 1024 * 1024

NT_DIMS = (((1,), (1,)), ((), ()))


def _cparams(*sem):
    return pltpu.CompilerParams(dimension_semantics=sem, vmem_limit_bytes=VMEM_LIMIT)


def _rms_rows(x, g):
    ms = jnp.mean(x * x, axis=-1, keepdims=True)
    return x * lax.rsqrt(ms + EPS) * g


def _sigmoid(x):
    return 1.0 / (1.0 + jnp.exp(-x))


def _qkv_body(x_ref, g_ref, wqk_ref, wv_ref, gq_ref, gk_ref, cos_ref, sin_ref,
              qT_ref, kT_ref, kTb_ref, v_ref, vb_ref, *rest, with_kmean):
    if with_kmean:
        kmean_ref, y_scr = rest
    else:
        (y_scr,) = rest
    hn = _rms_rows(x_ref[...], g_ref[...]).astype(BF16)
    v = jnp.dot(hn, wv_ref[...], preferred_element_type=F32)
    v_ref[...] = v
    vb_ref[...] = v.astype(BF16)
    y_scr[...] = lax.dot_general(wqk_ref[...], hn, NT_DIMS, preferred_element_type=F32)
    cos = cos_ref[...]
    sin = sin_ref[...]
    half = HEAD_DIM // 2
    n_groups = D_MODEL // HEAD_DIM
    for grp in range(2 * n_groups):
        y = y_scr[grp * HEAD_DIM:(grp + 1) * HEAD_DIM, :]
        gain = gq_ref[...] if grp < n_groups else gk_ref[...]
        ms = jnp.mean(y * y, axis=0, keepdims=True)
        y = y * lax.rsqrt(ms + EPS) * gain
        rot = jnp.concatenate([-y[half:, :], y[:half, :]], axis=0)
        y = y * cos + rot * sin
        if grp < n_groups:
            qT_ref[grp * HEAD_DIM:(grp + 1) * HEAD_DIM, :] = y
        else:
            r = (grp - n_groups) * HEAD_DIM
            kT_ref[r:r + HEAD_DIM, :] = y
            kTb_ref[r:r + HEAD_DIM, :] = y.astype(BF16)
    if with_kmean:
        t = x_ref.shape[0]
        avg = jnp.full((8, t), 1.0 / t, F32)
        km = lax.dot_general(avg, kT_ref[...], NT_DIMS, precision=HIGHEST,
                             preferred_element_type=F32)
        kmean_ref[0] = km[0:1, :]


def _qkv(x, g, wqkT, wv, gq, gk, cosT, sinT, *, tile, with_kmean):
    T = x.shape[0]
    assert T % tile == 0
    n = T // tile
    out_shape = [
        jax.ShapeDtypeStruct((D_MODEL, T), F32),
        jax.ShapeDtypeStruct((D_MODEL, T), F32),
        jax.ShapeDtypeStruct((D_MODEL, T), BF16),
        jax.ShapeDtypeStruct((T, D_MODEL), F32),
        jax.ShapeDtypeStruct((T, D_MODEL), BF16),
    ]
    colblk = pl.BlockSpec((D_MODEL, tile), lambda i: (0, i))
    rowblk = pl.BlockSpec((tile, D_MODEL), lambda i: (i, 0))
    out_specs = [colblk, colblk, colblk, rowblk, rowblk]
    if with_kmean:
        assert tile == MOBA_BLOCK
        out_shape.append(jax.ShapeDtypeStruct((n, 1, D_MODEL), F32))
        out_specs.append(pl.BlockSpec((1, 1, D_MODEL), lambda i: (i, 0, 0)))
    const2 = lambda i: (0, 0)
    return pl.pallas_call(
        functools.partial(_qkv_body, with_kmean=with_kmean),
        grid=(n,),
        in_specs=[
            rowblk,
            pl.BlockSpec((1, D_MODEL), const2),
            pl.BlockSpec((2 * D_MODEL, D_MODEL), const2),
            pl.BlockSpec((D_MODEL, D_MODEL), const2),
            pl.BlockSpec((HEAD_DIM, 1), const2),
            pl.BlockSpec((HEAD_DIM, 1), const2),
            pl.BlockSpec((HEAD_DIM, tile), lambda i: (0, i)),
            pl.BlockSpec((HEAD_DIM, tile), lambda i: (0, i)),
        ],
        out_specs=out_specs,
        out_shape=out_shape,
        scratch_shapes=[pltpu.VMEM((2 * D_MODEL, tile), F32)],
        compiler_params=_cparams("arbitrary"),
        name="qkv_norm_rope",
    )(x, g, wqkT, wv, gq, gk, cosT, sinT)


def _online_update(s, vv, m_ref, l_ref, acc_ref):
    m_old = m_ref[...]
    m_new = jnp.maximum(m_old, jnp.max(s, axis=1, keepdims=True))
    p = jnp.exp(s - m_new)
    alpha = jnp.exp(m_old - m_new)
    l_ref[...] = alpha * l_ref[...] + jnp.sum(p, axis=1, keepdims=True)
    acc_ref[...] = alpha * acc_ref[...] + jnp.dot(p.astype(BF16), vv, preferred_element_type=F32)
    m_ref[...] = m_new


def _init_softmax_state(m_ref, l_ref, acc_ref):
    m_ref[...] = jnp.full(m_ref.shape, MASK_NEG, F32)
    l_ref[...] = jnp.zeros(l_ref.shape, F32)
    acc_ref[...] = jnp.zeros(acc_ref.shape, F32)


LOG2E = math.log2(math.e)


def _flash_step(s, v_aug, m_ref, acc_ref):
    tk = s.shape[1]
    w = acc_ref.shape[1]
    m_old = m_ref[...]
    m_new = jnp.maximum(m_old, jnp.max(s, axis=1, keepdims=True))
    alpha = jnp.exp2(m_old - m_new)
    p = jnp.exp2(s - jnp.concatenate([m_new] * (tk // LANES), axis=1))
    pv = jnp.dot(p.astype(BF16), v_aug, preferred_element_type=F32)
    if w > LANES:
        alpha = jnp.concatenate([alpha] * (w // LANES), axis=1)
    acc_ref[...] = alpha * acc_ref[...] + pv
    m_ref[...] = m_new


def _init_flash_state(m_ref, acc_ref):
    m_ref[...] = jnp.full(m_ref.shape, MASK_NEG, F32)
    acc_ref[...] = jnp.zeros(acc_ref.shape, F32)


def _run_pipelined(n_full, scores, consume, s_a, s_b):
    scores(0, s_a)

    def pair_body(i, c):
        scores(2 * i + 1, s_b)
        consume(2 * i, s_a, False)
        scores(2 * i + 2, s_a)
        consume(2 * i + 1, s_b, False)
        return c

    lax.fori_loop(0, n_full // 2, pair_body, 0)

    @pl.when(n_full % 2 == 0)
    def _():
        consume(n_full, s_a, True)

    @pl.when(n_full % 2 == 1)
    def _():
        scores(n_full, s_b)
        consume(n_full - 1, s_a, False)
        consume(n_full, s_b, True)


def _diff_lambda(lam_ref, lam_init):
    lf = lam_ref[...]
    a = jnp.sum(lf[0:1, :] * lf[1:2, :], axis=1, keepdims=True)
    b = jnp.sum(lf[2:3, :] * lf[3:4, :], axis=1, keepdims=True)
    return jnp.exp(a) - jnp.exp(b) + lam_init


def _diff_attn_body(qT_ref, kT_ref, v_ref, lam_ref, gsub_ref, o_ref, m_scr, acc_scr, sa_scr, sb_scr,
                    *, tile, lam_init):
    qi = pl.program_id(1)
    hw = 2 * HEAD_DIM
    q = qT_ref[...].T * (HEAD_DIM ** -0.5 * LOG2E)
    lane = lax.broadcasted_iota(jnp.int32, q.shape, 1)
    q12 = jnp.concatenate([jnp.where(lane < HEAD_DIM, q, 0.0), jnp.where(lane >= HEAD_DIM, q, 0.0)],
                          axis=0).astype(BF16)
    _init_flash_state(m_scr, acc_scr)
    ones = jnp.ones((tile, hw), BF16)

    def scores(kb, s_ref):
        off = pl.multiple_of(kb * tile, tile)
        s_ref[...] = jnp.dot(q12, kT_ref[:, pl.ds(off, tile)], preferred_element_type=F32)

    def consume(kb, s_ref, masked):
        off = pl.multiple_of(kb * tile, tile)
        v_aug = jnp.concatenate([v_ref[pl.ds(off, tile), :], ones], axis=1)
        s = s_ref[...]
        if masked:
            row = jnp.bitwise_and(lax.broadcasted_iota(jnp.int32, s.shape, 0), tile - 1)
            col = lax.broadcasted_iota(jnp.int32, s.shape, 1)
            s = jnp.where(col <= row, s, MASK_NEG)
        _flash_step(s, v_aug, m_scr, acc_scr)

    _run_pipelined(qi, scores, consume, sa_scr, sb_scr)

    lam = _diff_lambda(lam_ref, lam_init)
    acc = acc_scr[...]
    o = acc[:, :hw] / acc[:, hw:]
    o = o[:tile, :] - lam * o[tile:, :]
    o = _rms_rows(o, gsub_ref[...]) * (1.0 - lam_init)
    o_ref[...] = o.astype(o_ref.dtype)


def _diff_attn_prompt(qT, kTb, vb, lam, gsub, *, lam_init, tile):
    T = qT.shape[1]
    hw = 2 * HEAD_DIM
    assert T % tile == 0
    return pl.pallas_call(
        functools.partial(_diff_attn_body, tile=tile, lam_init=lam_init),
        grid=(N_HEADS_A, T // tile),
        in_specs=[
            pl.BlockSpec((hw, tile), lambda h, i: (h, i)),
            pl.BlockSpec((hw, T), lambda h, i: (h, 0)),
            pl.BlockSpec((T, hw), lambda h, i: (0, h)),
            pl.BlockSpec((4, HEAD_DIM), lambda h, i: (0, 0)),
            pl.BlockSpec((1, hw), lambda h, i: (0, 0)),
        ],
        out_specs=pl.BlockSpec((tile, hw), lambda h, i: (i, h)),
        out_shape=jax.ShapeDtypeStruct((T, D_MODEL), BF16),
        scratch_shapes=[pltpu.VMEM((2 * tile, LANES), F32), pltpu.VMEM((2 * tile, 2 * hw), F32),
                        pltpu.VMEM((2 * tile, tile), F32), pltpu.VMEM((2 * tile, tile), F32)],
        compiler_params=_cparams("arbitrary", "arbitrary"),
        name="diff_attn_prompt",
    )(qT, kTb, vb, lam, gsub)


def _topk_pick(g, posf, k, axis=1):
    picked = jnp.zeros(g.shape, jnp.bool_)
    idxs = []
    for _ in range(k):
        m = jnp.max(g, axis=axis, keepdims=True)
        idx = jnp.min(jnp.where(g == m, posf, float(4 * LANES)), axis=axis, keepdims=True)
        hit = posf == idx
        picked = jnp.logical_or(picked, hit)
        g = jnp.where(hit, KNOCKED, g)
        idxs.append(idx)
    return picked, idxs


def _moba_body(qT_ref, kT_ref, v_ref, km_ref, o_ref, ma, aa, mb, ab, sa_scr, sb_scr, *, tile):
    qi = pl.program_id(1)
    blk_shift = MOBA_BLOCK.bit_length() - 1
    bpt = tile // MOBA_BLOCK
    qT = qT_ref[...]
    qsT = qT * (HEAD_DIM ** -0.5 * LOG2E)
    lo = lax.broadcasted_iota(jnp.int32, (tile, 2 * HEAD_DIM), 1) < HEAD_DIM
    blk = lax.broadcasted_iota(jnp.int32, (HEAD_DIM, tile), 0)
    blkf = blk.astype(F32)
    own = qi * bpt + jnp.right_shift(lax.broadcasted_iota(jnp.int32, (HEAD_DIM, tile), 1), blk_shift)
    past = blk < own

    def select_bias(hh):
        g = jnp.dot(km_ref[hh], qT[hh * HEAD_DIM:(hh + 1) * HEAD_DIM, :], precision=HIGHEST,
                    preferred_element_type=F32)
        g = jnp.where(past, g, MASK_NEG)
        picked, _ = _topk_pick(g, blkf, MOBA_TOPK, axis=0)
        sel = jnp.logical_or(jnp.logical_and(picked, past), blk == own)
        return jnp.where(sel, 0.0, -SEL_BIAS)

    qa = jnp.concatenate([qsT[:HEAD_DIM, :], select_bias(0)], axis=0).T
    qb = jnp.concatenate([select_bias(1), qsT[HEAD_DIM:, :]], axis=0).T
    zq = jnp.zeros_like(qa)
    qc = jnp.concatenate([jnp.concatenate([qa, zq], axis=1), jnp.concatenate([zq, qb], axis=1)],
                         axis=0).astype(BF16)
    _init_flash_state(ma, aa)
    _init_flash_state(mb, ab)
    sub = lax.broadcasted_iota(jnp.int32, (HEAD_DIM, tile), 0)
    key_blk = jnp.right_shift(lax.broadcasted_iota(jnp.int32, (HEAD_DIM, tile), 1), blk_shift)
    vlane = lax.broadcasted_iota(jnp.int32, (tile, 2 * HEAD_DIM), 1)
    one = jnp.ones((tile, 2 * HEAD_DIM), BF16)

    def scores(kb, s_ref):
        off = pl.multiple_of(kb * tile, tile)
        kt = kT_ref[:, pl.ds(off, tile)]
        onehot = jnp.where(sub == kb * bpt + key_blk, 1.0, 0.0).astype(BF16)
        kc = jnp.concatenate([kt[:HEAD_DIM, :], onehot, onehot, kt[HEAD_DIM:, :]], axis=0)
        s_ref[...] = jnp.dot(qc, kc, preferred_element_type=F32)

    def consume(kb, s_ref, masked):
        off = pl.multiple_of(kb * tile, tile)
        vv = v_ref[pl.ds(off, tile), :]
        sa = s_ref[:tile, :]
        sb = s_ref[tile:, :]
        if masked:
            row = lax.broadcasted_iota(jnp.int32, (tile, tile), 0)
            col = lax.broadcasted_iota(jnp.int32, (tile, tile), 1)
            keep = col <= row
            sa = jnp.where(keep, sa, MASK_NEG)
            sb = jnp.where(keep, sb, MASK_NEG)
        _flash_step(sa, jnp.where(vlane < HEAD_DIM, vv, one), ma, aa)
        _flash_step(sb, jnp.where(vlane >= HEAD_DIM, vv, one), mb, ab)

    _run_pipelined(qi, scores, consume, sa_scr, sb_scr)
    acc_a = aa[...]
    acc_b = ab[...]
    oa = acc_a / pltpu.roll(acc_a, HEAD_DIM, 1)
    ob = acc_b / pltpu.roll(acc_b, HEAD_DIM, 1)
    o_ref[...] = jnp.where(lo, oa, ob).astype(o_ref.dtype)


def _moba_prompt(qT, kTb, vb, km_aug, *, tile):
    T = qT.shape[1]
    hw = 2 * HEAD_DIM
    assert T % tile == 0 and tile % MOBA_BLOCK == 0 and T // MOBA_BLOCK <= HEAD_DIM
    return pl.pallas_call(
        functools.partial(_moba_body, tile=tile),
        grid=(N_HEADS_B // 2, T // tile),
        in_specs=[
            pl.BlockSpec((hw, tile), lambda h, i: (h, i)),
            pl.BlockSpec((hw, T), lambda h, i: (h, 0)),
            pl.BlockSpec((T, hw), lambda h, i: (0, h)),
            pl.BlockSpec((2, HEAD_DIM, HEAD_DIM), lambda h, i: (h, 0, 0)),
        ],
        out_specs=pl.BlockSpec((tile, hw), lambda h, i: (i, h)),
        out_shape=jax.ShapeDtypeStruct((T, D_MODEL), BF16),
        scratch_shapes=[
            pltpu.VMEM((tile, LANES), F32), pltpu.VMEM((tile, hw), F32),
            pltpu.VMEM((tile, LANES), F32), pltpu.VMEM((tile, hw), F32),
            pltpu.VMEM((2 * tile, tile), F32), pltpu.VMEM((2 * tile, tile), F32),
        ],
        compiler_params=_cparams("arbitrary", "arbitrary"),
        name="moba_attn_prompt",
    )(qT, kTb, vb, km_aug)


def _moba_kmean_per_head(kmean):
    n_blk = kmean.shape[0]
    km = kmean.reshape(n_blk, N_HEADS_B, HEAD_DIM).transpose(1, 0, 2)
    return jnp.pad(km, ((0, 0), (0, HEAD_DIM - n_blk), (0, 0)))


def _oproj_body(x_ref, o_ref, w_ref, y_ref):
    y_ref[...] = x_ref[...] + jnp.dot(o_ref[...], w_ref[...], preferred_element_type=F32)


def _oproj(x, o, w, *, tile):
    T = x.shape[0]
    rowblk = pl.BlockSpec((tile, D_MODEL), lambda i: (i, 0))
    return pl.pallas_call(
        _oproj_body,
        grid=(T // tile,),
        in_specs=[rowblk, rowblk, pl.BlockSpec((D_MODEL, D_MODEL), lambda i: (0, 0))],
        out_specs=rowblk,
        out_shape=jax.ShapeDtypeStruct((T, D_MODEL), F32),
        compiler_params=_cparams("arbitrary"),
        name="attn_out_proj",
    )(x, o, w)


def _swiglu_chunk(hn, wg, wu, wd):
    a = jnp.dot(hn, wg, preferred_element_type=F32)
    u = jnp.dot(hn, wu, preferred_element_type=F32)
    act = (a * _sigmoid(a) * u).astype(BF16)
    return jnp.dot(act, wd, preferred_element_type=F32)


def _ffn_body(x_ref, g_ref, wg_ref, wu_ref, wd_ref, y_ref, hn_scr, acc_scr):
    f = pl.program_id(1)

    @pl.when(f == 0)
    def _():
        x = x_ref[...]
        hn_scr[...] = _rms_rows(x, g_ref[...]).astype(BF16)
        acc_scr[...] = x

    acc_scr[...] += _swiglu_chunk(hn_scr[...], wg_ref[...], wu_ref[...], wd_ref[...])

    @pl.when(f == pl.num_programs(1) - 1)
    def _():
        y_ref[...] = acc_scr[...]


def _ffn(x, g, wg, wu, wd, *, tile, tf):
    T = x.shape[0]
    F = wg.shape[1]
    assert F % tf == 0
    rowblk = pl.BlockSpec((tile, D_MODEL), lambda i, f: (i, 0))
    return pl.pallas_call(
        _ffn_body,
        grid=(T // tile, F // tf),
        in_specs=[
            rowblk,
            pl.BlockSpec((1, D_MODEL), lambda i, f: (0, 0)),
            pl.BlockSpec((D_MODEL, tf), lambda i, f: (0, f)),
            pl.BlockSpec((D_MODEL, tf), lambda i, f: (0, f)),
            pl.BlockSpec((tf, D_MODEL), lambda i, f: (f, 0)),
        ],
        out_specs=rowblk,
        out_shape=jax.ShapeDtypeStruct((T, D_MODEL), F32),
        scratch_shapes=[pltpu.VMEM((tile, D_MODEL), BF16), pltpu.VMEM((tile, D_MODEL), F32)],
        compiler_params=_cparams("arbitrary", "arbitrary"),
        name="dense_swiglu",
    )(x, g, wg, wu, wd)


ROUTE_MASK = 8
ROUTE_RANK = 16


def _router_body(x_ref, g_ref, wr_ref, hn_ref, route_ref, routeT_ref, count_ref):
    hf = _rms_rows(x_ref[...], g_ref[...])
    hn_ref[...] = hf.astype(BF16)
    logits = jnp.dot(hf, wr_ref[...], precision=HIGHEST, preferred_element_type=F32)
    lane = lax.broadcasted_iota(jnp.int32, logits.shape, 1)
    lanef = lane.astype(F32)
    logits = jnp.where(lane < N_EXPERTS, logits, MASK_NEG)
    v1 = jnp.max(logits, axis=1, keepdims=True)
    i1 = jnp.min(jnp.where(logits == v1, lanef, float(LANES)), axis=1, keepdims=True)
    rest = jnp.where(lanef == i1, KNOCKED, logits)
    v2 = jnp.max(rest, axis=1, keepdims=True)
    i2 = jnp.min(jnp.where(rest == v2, lanef, float(LANES)), axis=1, keepdims=True)
    e2 = jnp.exp(v2 - v1)
    g1 = 1.0 / (1.0 + e2)
    g2 = e2 / (1.0 + e2)
    weights = jnp.where(lanef == i1, g1, 0.0) + jnp.where(lanef == i2, g2, 0.0)
    le = lanef - float(ROUTE_RANK)
    routed = jnp.where(jnp.logical_or(le == i1, le == i2), 1.0, 0.0)
    t = hf.shape[0]
    before = (lax.broadcasted_iota(jnp.int32, (t, t), 1) < lax.broadcasted_iota(jnp.int32, (t, t), 0))
    rank = jnp.dot(jnp.where(before, 1.0, 0.0).astype(BF16), routed.astype(BF16),
                   preferred_element_type=F32)
    route = weights + pltpu.roll(routed, LANES - (ROUTE_RANK - ROUTE_MASK), 1) + rank * routed
    route_ref[...] = route
    eye = jnp.where(lax.broadcasted_iota(jnp.int32, (LANES, LANES), 0)
                    == lax.broadcasted_iota(jnp.int32, (LANES, LANES), 1), 1.0, 0.0)
    routeT_ref[...] = lax.dot_general(eye, route, NT_DIMS, precision=HIGHEST,
                                      preferred_element_type=F32)
    count_ref[0] = jnp.sum(routed, axis=0, keepdims=True).astype(jnp.int32)


def _router(x, g, wr, *, tile):
    T = x.shape[0]
    n = T // tile
    rowblk = pl.BlockSpec((tile, D_MODEL), lambda i: (i, 0))
    return pl.pallas_call(
        _router_body,
        grid=(n,),
        in_specs=[rowblk, pl.BlockSpec((1, D_MODEL), lambda i: (0, 0)),
                  pl.BlockSpec((D_MODEL, LANES), lambda i: (0, 0))],
        out_specs=[rowblk, pl.BlockSpec((tile, LANES), lambda i: (i, 0)),
                   pl.BlockSpec((LANES, tile), lambda i: (0, i)),
                   pl.BlockSpec((1, 1, LANES), lambda i: (i, 0, 0))],
        out_shape=[jax.ShapeDtypeStruct((T, D_MODEL), BF16), jax.ShapeDtypeStruct((T, LANES), F32),
                   jax.ShapeDtypeStruct((LANES, T), F32), jax.ShapeDtypeStruct((n, 1, LANES), jnp.int32)],
        compiler_params=_cparams("arbitrary"),
        name="moe_router",
    )(x, g, wr)


def _experts_body(cnt_ref, hn_ref, route_ref, routeT_ref, wg_ref, wu_ref, wd_ref, y_ref,
                  x_scr, y_scr, *, cap):
    i = pl.program_id(0)
    e = pl.program_id(1)
    f = pl.program_id(2)
    last_f = f == pl.num_programs(2) - 1
    t = hn_ref.shape[0]
    n_chunks = lax.div(cnt_ref[i, e] + (cap - 1), cap)

    @pl.when(jnp.logical_and(e == 0, f == 0))
    def _():
        y_ref[...] = jnp.zeros(y_ref.shape, F32)

    @pl.when(f == 0)
    def _():
        rankT = routeT_ref[pl.ds(ROUTE_RANK + e, 1), :]
        maskT = routeT_ref[pl.ds(ROUTE_MASK + e, 1), :]
        rankT = jnp.where(maskT > 0.5, rankT, -1.0)
        slot = lax.broadcasted_iota(jnp.int32, (cap, t), 0).astype(F32)

        def gather(c, carry):
            r0 = pl.multiple_of(c * cap, 16)
            onehot = jnp.where(rankT - (c * cap).astype(F32) == slot, 1.0, 0.0).astype(BF16)
            x_scr[pl.ds(r0, cap), :] = jnp.dot(onehot, hn_ref[...],
                                               preferred_element_type=F32).astype(BF16)
            y_scr[pl.ds(r0, cap), :] = jnp.zeros((cap, D_MODEL), F32)
            return carry

        lax.fori_loop(0, n_chunks, gather, 0)

    def expert(c, carry):
        r0 = pl.multiple_of(c * cap, 16)
        y_scr[pl.ds(r0, cap), :] += _swiglu_chunk(x_scr[pl.ds(r0, cap), :], wg_ref[...], wu_ref[...],
                                                  wd_ref[...])
        return carry

    lax.fori_loop(0, n_chunks, expert, 0)

    @pl.when(last_f)
    def _():
        route = route_ref[...]
        lane = lax.broadcasted_iota(jnp.int32, route.shape, 1)
        pick = lambda off: jnp.sum(jnp.where(lane == off + e, route, 0.0), axis=1, keepdims=True)
        w_e = pick(0)
        rank = jnp.where(pick(ROUTE_MASK) > 0.5, pick(ROUTE_RANK), -1.0)
        slot = lax.broadcasted_iota(jnp.int32, (t, 2 * cap), 1)
        slot = jnp.where(slot < cap, slot, slot - cap).astype(F32)

        def scatter(c, carry):
            r0 = pl.multiple_of(c * cap, 16)
            onehot = jnp.where(rank - (c * cap).astype(F32) == slot, 1.0, 0.0).astype(BF16)
            y = y_scr[pl.ds(r0, cap), :]
            y_hi = y.astype(BF16)
            y_lo = (y - y_hi.astype(F32)).astype(BF16)
            back = jnp.dot(onehot, jnp.concatenate([y_hi, y_lo], axis=0),
                           preferred_element_type=F32)
            y_ref[...] += w_e * back
            return carry

        lax.fori_loop(0, n_chunks, scatter, 0)


def _experts(counts, hn, route, routeT, wg, wu, wd, *, tile, tf, cap):
    T = hn.shape[0]
    E, _, F = wg.shape
    assert F % tf == 0 and cap % 16 == 0
    rows = -(-tile // cap) * cap
    rowblk = pl.BlockSpec((tile, D_MODEL), lambda i, e, f, cnt: (i, 0))
    grid_spec = pltpu.PrefetchScalarGridSpec(
        num_scalar_prefetch=1,
        grid=(T // tile, E, F // tf),
        in_specs=[
            rowblk,
            pl.BlockSpec((tile, LANES), lambda i, e, f, cnt: (i, 0)),
            pl.BlockSpec((LANES, tile), lambda i, e, f, cnt: (0, i)),
            pl.BlockSpec((None, D_MODEL, tf), lambda i, e, f, cnt: (e, 0, f)),
            pl.BlockSpec((None, D_MODEL, tf), lambda i, e, f, cnt: (e, 0, f)),
            pl.BlockSpec((None, tf, D_MODEL), lambda i, e, f, cnt: (e, f, 0)),
        ],
        out_specs=rowblk,
        scratch_shapes=[pltpu.VMEM((rows, D_MODEL), BF16), pltpu.VMEM((rows, D_MODEL), F32)],
    )
    return pl.pallas_call(
        functools.partial(_experts_body, cap=cap),
        grid_spec=grid_spec,
        out_shape=jax.ShapeDtypeStruct((T, D_MODEL), F32),
        compiler_params=_cparams("arbitrary", "arbitrary", "arbitrary"),
        name="moe_experts",
    )(counts, hn, route, routeT, wg, wu, wd)


def _ple_body(x_ref, *rest, with_addend):
    if with_addend:
        add_ref, g_ref, wgate_ref, p_ref, win_ref, y_ref = rest
        x = x_ref[...] + add_ref[...]
    else:
        g_ref, wgate_ref, p_ref, win_ref, y_ref = rest
        x = x_ref[...]
    hn = _rms_rows(x, g_ref[...]).astype(BF16)
    gate = _sigmoid(jnp.dot(hn, wgate_ref[...], preferred_element_type=F32))
    emb = jnp.dot(p_ref[...].astype(BF16), win_ref[...], preferred_element_type=F32)
    y_ref[...] = x + gate * emb


def _ple(x, addend, g, wgate, p, win, *, tile):
    T = x.shape[0]
    pd = p.shape[1]
    rowblk = pl.BlockSpec((tile, D_MODEL), lambda i: (i, 0))
    acts = (x,) if addend is None else (x, addend)
    return pl.pallas_call(
        functools.partial(_ple_body, with_addend=addend is not None),
        grid=(T // tile,),
        in_specs=[rowblk] * len(acts) + [
            pl.BlockSpec((1, D_MODEL), lambda i: (0, 0)),
            pl.BlockSpec((D_MODEL, D_MODEL), lambda i: (0, 0)),
            pl.BlockSpec((tile, pd), lambda i: (i, 0)),
            pl.BlockSpec((pd, D_MODEL), lambda i: (0, 0)),
        ],
        out_specs=rowblk,
        out_shape=jax.ShapeDtypeStruct((T, D_MODEL), F32),
        compiler_params=_cparams("arbitrary"),
        name="per_layer_embedding",
    )(*acts, g, wgate, p, win)


PAGES_PER_STEP = 16


def _diff_decode_body(pt_ref, qbd_ref, knew_ref, vnew_ref, lam_ref, gsub_ref, *rest,
                      dec_seq, lam_init):
    kT_refs = rest[:PAGES_PER_STEP]
    v_refs = rest[PAGES_PER_STEP:2 * PAGES_PER_STEP]
    o_ref, m_ref, l_ref, acc_ref = rest[2 * PAGES_PER_STEP:]
    j = pl.program_id(1)
    n_rows = qbd_ref.shape[0]

    @pl.when(j == 0)
    def _():
        _init_softmax_state(m_ref, l_ref, acc_ref)

    qbd = qbd_ref[...]
    s = jnp.concatenate(
        [jnp.dot(qbd, r[...].astype(BF16), preferred_element_type=F32) for r in kT_refs], axis=1)
    m_old = m_ref[...]
    m_new = jnp.maximum(m_old, jnp.max(s, axis=1, keepdims=True))
    p = jnp.exp(s - m_new)
    alpha = jnp.exp(m_old - m_new)
    l_ref[...] = alpha * l_ref[...] + jnp.sum(p, axis=1, keepdims=True)
    acc = alpha * acc_ref[...]
    for i, r in enumerate(v_refs):
        vcat = jnp.concatenate(
            [r[pl.ds(h, PAGE_SIZE, stride=N_HEADS_A), :] for h in range(N_HEADS_A)], axis=1)
        acc += jnp.dot(p[:, i * PAGE_SIZE:(i + 1) * PAGE_SIZE].astype(BF16), vcat.astype(BF16),
                       preferred_element_type=F32)
    acc_ref[...] = acc
    m_ref[...] = m_new

    @pl.when(j == pl.num_programs(1) - 1)
    def _():
        sn = lax.dot_general(qbd, knew_ref[...], NT_DIMS, preferred_element_type=F32)
        row = lax.broadcasted_iota(jnp.int32, sn.shape, 0)
        col = lax.broadcasted_iota(jnp.int32, sn.shape, 1)
        half = n_rows // 2
        t_of_row = jnp.right_shift(jnp.bitwise_and(row, half - 1), 3)
        sn = jnp.where(jnp.logical_and(col <= t_of_row, col < dec_seq), sn, MASK_NEG)
        _online_update(sn, vnew_ref[...], m_ref, l_ref, acc_ref)
        o = acc_ref[...] / l_ref[...]
        lam = _diff_lambda(lam_ref, lam_init)
        od = o[:half, :] - lam * o[half:, :]
        r2 = lax.broadcasted_iota(jnp.int32, od.shape, 0)
        c2 = lax.broadcasted_iota(jnp.int32, od.shape, 1)
        own = jnp.right_shift(c2, 7) == jnp.bitwise_and(r2, N_HEADS_A - 1)
        od = jnp.where(own, od, 0.0)
        ms = jnp.sum(od * od, axis=1, keepdims=True) / (2 * HEAD_DIM)
        od = od * lax.rsqrt(ms + EPS) * gsub_ref[...] * (1.0 - lam_init)
        rows = [jnp.sum(od[t * N_HEADS_A:(t + 1) * N_HEADS_A, :], axis=0, keepdims=True)
                for t in range(dec_seq)]
        o_ref[...] = jnp.concatenate(rows, axis=0)


def _diff_decode(page_table, qbd, knew, vnew, lam, gsub_tiled, kT_pages, v_rows, *, dec_seq,
                 lam_init):
    nb, n_pages = page_table.shape
    assert n_pages % PAGES_PER_STEP == 0
    n_rows = qbd.shape[1]
    kv_rows = PAGE_SIZE * N_HEADS_A

    def k_spec(i):
        return pl.BlockSpec((None, D_MODEL, PAGE_SIZE),
                            lambda b, j, pt: (pt[b, j * PAGES_PER_STEP + i], 0, 0))

    def v_spec(i):
        return pl.BlockSpec((kv_rows, 2 * HEAD_DIM),
                            lambda b, j, pt: (pt[b, j * PAGES_PER_STEP + i], 0))

    per_seq = lambda b, j, pt: (b, 0, 0)
    const2 = lambda b, j, pt: (0, 0)
    grid_spec = pltpu.PrefetchScalarGridSpec(
        num_scalar_prefetch=1,
        grid=(nb, n_pages // PAGES_PER_STEP),
        in_specs=[
            pl.BlockSpec((None, n_rows, D_MODEL), per_seq),
            pl.BlockSpec((None, PAGE_SIZE, D_MODEL), per_seq),
            pl.BlockSpec((None, PAGE_SIZE, D_MODEL), per_seq),
            pl.BlockSpec((4, HEAD_DIM), const2),
            pl.BlockSpec((1, D_MODEL), const2),
        ] + [k_spec(i) for i in range(PAGES_PER_STEP)] + [v_spec(i) for i in range(PAGES_PER_STEP)],
        out_specs=pl.BlockSpec((None, dec_seq, D_MODEL), per_seq),
        scratch_shapes=[pltpu.VMEM((n_rows, 1), F32), pltpu.VMEM((n_rows, 1), F32),
                        pltpu.VMEM((n_rows, D_MODEL), F32)],
    )
    return pl.pallas_call(
        functools.partial(_diff_decode_body, dec_seq=dec_seq, lam_init=lam_init),
        grid_spec=grid_spec,
        out_shape=jax.ShapeDtypeStruct((nb, dec_seq, D_MODEL), F32),
        compiler_params=_cparams("arbitrary", "arbitrary"),
        name="diff_attn_decode",
    )(page_table, qbd, knew, vnew, lam, gsub_tiled,
      *([kT_pages] * PAGES_PER_STEP), *([v_rows] * PAGES_PER_STEP))


def _moba_gate_body(pt_ref, qbd_ref, *rest):
    kT_refs = rest[:PAGES_PER_STEP]
    idx_ref, km_scr = rest[PAGES_PER_STEP:]
    j = pl.program_id(1)
    pages_per_block = MOBA_BLOCK // PAGE_SIZE
    blocks_per_step = PAGES_PER_STEP // pages_per_block

    @pl.when(j == 0)
    def _():
        km_scr[...] = jnp.zeros(km_scr.shape, F32)

    lane = lax.broadcasted_iota(jnp.int32, km_scr.shape, 1)
    km = km_scr[...]
    for bb in range(blocks_per_step):
        pages = kT_refs[bb * pages_per_block][...]
        for i in range(1, pages_per_block):
            pages = pages + kT_refs[bb * pages_per_block + i][...]
        tot = jnp.sum(pages, axis=1, keepdims=True)
        km = jnp.where(lane == j * blocks_per_step + bb, tot * (1.0 / MOBA_BLOCK), km)
    km_scr[...] = km

    @pl.when(j == pl.num_programs(1) - 1)
    def _():
        n_blk = pl.num_programs(1) * blocks_per_step
        g = jnp.dot(qbd_ref[...], km_scr[...], precision=HIGHEST, preferred_element_type=F32)
        glane = lax.broadcasted_iota(jnp.int32, g.shape, 1)
        glanef = glane.astype(F32)
        g = jnp.where(glane < n_blk, g, MASK_NEG)
        _, idxs = _topk_pick(g, glanef, MOBA_TOPK)
        out = jnp.zeros(g.shape, F32)
        for r, idx in enumerate(idxs):
            out = jnp.where(glane == r, idx, out)
        idx_ref[...] = out.astype(jnp.int32)


def _moba_gate(page_table, qbd, kT_pages):
    nb, n_pages = page_table.shape
    assert n_pages % PAGES_PER_STEP == 0 and n_pages * PAGE_SIZE // MOBA_BLOCK <= LANES
    n_rows = qbd.shape[1]

    def k_spec(i):
        return pl.BlockSpec((None, D_MODEL, PAGE_SIZE),
                            lambda b, j, pt: (pt[b, j * PAGES_PER_STEP + i], 0, 0))

    per_seq = lambda b, j, pt: (b, 0, 0)
    grid_spec = pltpu.PrefetchScalarGridSpec(
        num_scalar_prefetch=1,
        grid=(nb, n_pages // PAGES_PER_STEP),
        in_specs=[pl.BlockSpec((None, n_rows, D_MODEL), per_seq)]
        + [k_spec(i) for i in range(PAGES_PER_STEP)],
        out_specs=pl.BlockSpec((None, n_rows, LANES), per_seq),
        scratch_shapes=[pltpu.VMEM((D_MODEL, LANES), F32)],
    )
    return pl.pallas_call(
        _moba_gate_body,
        grid_spec=grid_spec,
        out_shape=jax.ShapeDtypeStruct((nb, n_rows, LANES), jnp.int32),
        compiler_params=_cparams("arbitrary", "arbitrary"),
        name="moba_decode_gate",
    )(page_table, qbd, *([kT_pages] * PAGES_PER_STEP))


def _moba_sel_body(phys_ref, q_ref, knew_ref, vnew_ref, *rest, dec_seq, n_sel):
    k_refs = rest[:dec_seq * n_sel]
    v_refs = rest[dec_seq * n_sel:2 * dec_seq * n_sel]
    (o_ref,) = rest[2 * dec_seq * n_sel:]
    q = q_ref[...]
    lane = lax.broadcasted_iota(jnp.int32, (1, PAGE_SIZE), 1)
    olane = lax.broadcasted_iota(jnp.int32, o_ref.shape, 1)
    out = jnp.zeros(o_ref.shape, F32)
    for t in range(dec_seq):
        qcol = q[:, t:t + 1]
        ks = [k_refs[t * n_sel + s][...] for s in range(n_sel)] + [knew_ref[...]]
        vs = [v_refs[t * n_sel + s][...] for s in range(n_sel)] + [vnew_ref[...]]
        scores = [jnp.sum(k * qcol, axis=0, keepdims=True) for k in ks]
        scores[-1] = jnp.where(lane <= t, scores[-1], MASK_NEG)
        m = scores[0].max(axis=1, keepdims=True)
        for sc in scores[1:]:
            m = jnp.maximum(m, sc.max(axis=1, keepdims=True))
        ps = [jnp.exp(sc - m) for sc in scores]
        l = ps[0].sum(axis=1, keepdims=True)
        for pp in ps[1:]:
            l = l + pp.sum(axis=1, keepdims=True)
        o = jnp.sum(vs[0] * ps[0], axis=1, keepdims=True)
        for vv, pp in zip(vs[1:], ps[1:]):
            o = o + jnp.sum(vv * pp, axis=1, keepdims=True)
        out = jnp.where(olane == t, o / l, out)
    o_ref[...] = out


def _moba_decode_attend(phys, qT, knewT, vnewT, kT_slices, vT_slices, *, dec_seq):
    nb, nh, n_slices = phys.shape
    n_sel = n_slices // dec_seq

    def sel_spec(s):
        return pl.BlockSpec((None, None, HEAD_DIM, PAGE_SIZE),
                            lambda b, h, ph: (ph[b, h, s], h, 0, 0))

    per_head = pl.BlockSpec((None, None, HEAD_DIM, PAGE_SIZE), lambda b, h, ph: (b, h, 0, 0))
    grid_spec = pltpu.PrefetchScalarGridSpec(
        num_scalar_prefetch=1,
        grid=(nb, nh),
        in_specs=[per_head, per_head, per_head]
        + [sel_spec(s) for s in range(n_slices)] + [sel_spec(s) for s in range(n_slices)],
        out_specs=per_head,
    )
    return pl.pallas_call(
        functools.partial(_moba_sel_body, dec_seq=dec_seq, n_sel=n_sel),
        grid_spec=grid_spec,
        out_shape=jax.ShapeDtypeStruct((nb, nh, HEAD_DIM, PAGE_SIZE), F32),
        compiler_params=_cparams("arbitrary", "arbitrary"),
        name="moba_decode_attend",
    )(phys, qT, knewT, vnewT, *([kT_slices] * n_slices), *([vT_slices] * n_slices))


def _rope_tables(pos):
    inv = ROPE_THETA ** (-jnp.arange(0, HEAD_DIM, 2, dtype=F32) / HEAD_DIM)
    ang = pos.astype(F32)[:, None] * inv[None, :]
    ang = jnp.concatenate([ang, ang], axis=-1)
    return jnp.cos(ang).T, jnp.sin(ang).T


def _col(v):
    return v.reshape(-1, 1)


def _row(v):
    return v.reshape(1, -1)


def _pad_lanes(a, n):
    return jnp.pad(a, [(0, 0)] * (a.ndim - 1) + [(0, n - a.shape[-1])])


def kernel(x_prompt, x_sample, cache_k_a, cache_v_a, cache_k_b, cache_v_b, page_table, p_prompt, p_sample,
           g_mix, g_ffn, g_ple, w_ple_gate, w_ple_in, w_qkv_a, g_q_a, g_k_a, lam_a, g_sub_a, w_o_a,
           w_qkv_b, g_q_b, g_k_b, w_o_b, w_ff_gate, w_ff_up, w_ff_down, w_router, w_ex_gate, w_ex_up, w_ex_down):
    _, T, D = x_prompt.shape
    nb, dec_seq, _ = x_sample.shape
    n_dec = nb * dec_seq
    n_pool = cache_k_a.shape[1]
    past_len = page_table.shape[1] * PAGE_SIZE
    assert D == D_MODEL and n_dec % 8 == 0 and dec_seq <= PAGE_SIZE
    pt = page_table.astype(jnp.int32)

    tile_p = 512 if T % 512 == 0 else MOBA_BLOCK
    tile_a = 1024 if T % 1024 == 0 else tile_p
    xp = x_prompt.reshape(T, D)
    xs = x_sample.reshape(n_dec, D)
    cos_p, sin_p = _rope_tables(jnp.arange(T))
    pos_s = past_len + jnp.tile(jnp.arange(dec_seq), nb)
    cos_s, sin_s = _rope_tables(pos_s)

    lam_init = 0.8 - 0.6 * math.exp(-0.3 * 0)
    wqkT = w_qkv_a[0][:, :2 * D].T.astype(BF16)
    wv = w_qkv_a[0][:, 2 * D:].astype(BF16)
    wo = w_o_a[0].astype(BF16)
    qkv_args = (_row(g_mix[0]), wqkT, wv, _col(g_q_a[0]), _col(g_k_a[0]))
    qT_p, kT_p, kTb_p, v_p, vb_p = _qkv(xp, *qkv_args, cos_p, sin_p, tile=MOBA_BLOCK, with_kmean=False)
    qT_s, kT_s, _, v_s, _ = _qkv(xs, *qkv_args, cos_s, sin_s, tile=n_dec, with_kmean=False)
    ka_p = kT_p.reshape(1, 1, N_HEADS_A, 2, HEAD_DIM, T).transpose(0, 1, 5, 2, 3, 4)
    va_p = v_p.reshape(1, 1, T, N_HEADS_A, 2 * HEAD_DIM)
    ka_s = kT_s.T.reshape(1, nb, dec_seq, N_HEADS_A, 2, HEAD_DIM)
    va_s = v_s.reshape(1, nb, dec_seq, N_HEADS_A, 2 * HEAD_DIM)

    o_p = _diff_attn_prompt(qT_p, kTb_p, vb_p, lam_a[0], _row(g_sub_a[0]), lam_init=lam_init,
                            tile=tile_a)

    q5 = (qT_s.T * (HEAD_DIM ** -0.5)).reshape(nb, dec_seq, N_HEADS_A, 2, HEAD_DIM)
    eye_h = jnp.eye(N_HEADS_A, dtype=F32)
    eye_c = jnp.eye(2, dtype=F32)
    qbd = jnp.einsum('bthcd,hH,cC->bcthHCd', q5, eye_h, eye_c).reshape(nb, 2 * dec_seq * N_HEADS_A, D)
    qbd = qbd.astype(BF16)
    knew = jnp.pad(kT_s.T.reshape(nb, dec_seq, D), ((0, 0), (0, PAGE_SIZE - dec_seq), (0, 0))).astype(BF16)
    vnew = jnp.pad(v_s.reshape(nb, dec_seq, D), ((0, 0), (0, PAGE_SIZE - dec_seq), (0, 0))).astype(BF16)
    kT_pages_a = cache_k_a[0].transpose(0, 2, 3, 4, 1).reshape(n_pool, D, PAGE_SIZE)
    v_rows_a = cache_v_a[0].reshape(n_pool * PAGE_SIZE * N_HEADS_A, 2 * HEAD_DIM)
    o_s = _diff_decode(pt, qbd, knew, vnew, lam_a[0], _row(jnp.tile(g_sub_a[0], N_HEADS_A)),
                       kT_pages_a, v_rows_a, dec_seq=dec_seq, lam_init=lam_init)
    o_s = o_s.reshape(n_dec, D).astype(BF16)

    xp = _oproj(xp, o_p, wo, tile=tile_p)
    xs = _oproj(xs, o_s, wo, tile=n_dec)

    wg = w_ff_gate[0].astype(BF16)
    wu = w_ff_up[0].astype(BF16)
    wd = w_ff_down[0].astype(BF16)
    tf = wg.shape[1] // 2
    xp = _ffn(xp, _row(g_ffn[0]), wg, wu, wd, tile=tile_p, tf=tf)
    xs = _ffn(xs, _row(g_ffn[0]), wg, wu, wd, tile=n_dec, tf=tf)

    wpg = w_ple_gate[0].astype(BF16)
    wpi = w_ple_in[0].astype(BF16)
    xp = _ple(xp, None, _row(g_ple[0]), wpg, p_prompt[0].reshape(T, -1), wpi, tile=tile_p)
    xs = _ple(xs, None, _row(g_ple[0]), wpg, p_sample[0].reshape(n_dec, -1), wpi, tile=n_dec)

    wqkT = w_qkv_b[0][:, :2 * D].T.astype(BF16)
    wv = w_qkv_b[0][:, 2 * D:].astype(BF16)
    wo = w_o_b[0].astype(BF16)
    qkv_args = (_row(g_mix[1]), wqkT, wv, _col(g_q_b[0]), _col(g_k_b[0]))
    qT_p, kT_p, kTb_p, v_p, vb_p, kmean = _qkv(xp, *qkv_args, cos_p, sin_p, tile=MOBA_BLOCK, with_kmean=True)
    qT_s, kT_s, _, v_s, _ = _qkv(xs, *qkv_args, cos_s, sin_s, tile=n_dec, with_kmean=False)
    kb_p = kT_p.reshape(1, 1, N_HEADS_B, HEAD_DIM, T).transpose(0, 1, 4, 2, 3)
    vb_out_p = v_p.reshape(1, 1, T, N_HEADS_B, HEAD_DIM)
    kb_s = kT_s.T.reshape(1, nb, dec_seq, N_HEADS_B, HEAD_DIM)
    vb_out_s = v_s.reshape(1, nb, dec_seq, N_HEADS_B, HEAD_DIM)

    o_p = _moba_prompt(qT_p, kTb_p, vb_p, _moba_kmean_per_head(kmean), tile=tile_a)

    q4 = qT_s.T.reshape(nb, dec_seq, N_HEADS_B, HEAD_DIM)
    qbd2 = jnp.einsum('bthd,hH->bhtHd', q4, jnp.eye(N_HEADS_B, dtype=F32)).reshape(nb, N_HEADS_B * dec_seq, D)
    kT_pages_b = cache_k_b[0].transpose(0, 2, 3, 1).reshape(n_pool, D, PAGE_SIZE)
    idx = _moba_gate(pt, qbd2, kT_pages_b)[:, :, :MOBA_TOPK]
    idx = idx.reshape(nb, N_HEADS_B, dec_seq, MOBA_TOPK)
    ppb = MOBA_BLOCK // PAGE_SIZE
    logical = (idx[..., None] * ppb + jnp.arange(ppb)).reshape(nb, N_HEADS_B, -1)
    per_head = lambda aT: aT.reshape(N_HEADS_B, HEAD_DIM, nb, dec_seq).transpose(2, 0, 1, 3)
    qT_h = _pad_lanes(per_head(qT_s) * (HEAD_DIM ** -0.5), PAGE_SIZE)
    knewT = _pad_lanes(per_head(kT_s), PAGE_SIZE)
    vnewT = _pad_lanes(per_head(v_s.T), PAGE_SIZE)
    kT_slices_b = cache_k_b[0].transpose(0, 2, 3, 1)
    vT_slices_b = cache_v_b[0].transpose(0, 2, 3, 1)
    hit = logical[..., None] == jnp.arange(pt.shape[1], dtype=jnp.int32)
    phys = jnp.sum(jnp.where(hit, pt[:, None, None, :], 0), axis=-1)
    oT = _moba_decode_attend(phys, qT_h, knewT, vnewT, kT_slices_b, vT_slices_b, dec_seq=dec_seq)
    o_s = oT[..., :dec_seq].transpose(0, 3, 1, 2).reshape(n_dec, D).astype(BF16)

    xp = _oproj(xp, o_p, wo, tile=tile_p)
    xs = _oproj(xs, o_s, wo, tile=n_dec)

    wr = _pad_lanes(w_router[0], LANES)
    weg = w_ex_gate[0].astype(BF16)
    weu = w_ex_up[0].astype(BF16)
    wed = w_ex_down[0].astype(BF16)
    tfe = weg.shape[2] // 4
    def moe(x, tile, cap):
        hn, route, routeT, cnt = _router(x, _row(g_ffn[1]), wr, tile=tile)
        counts = cnt[:, 0, ROUTE_RANK:ROUTE_RANK + N_EXPERTS]
        return _experts(counts, hn, route, routeT, weg, weu, wed, tile=tile, tf=tfe, cap=cap)

    tile_e = 1024 if T % 1024 == 0 else tile_p
    moe_p = moe(xp, tile_e, (5 * tile_e // 16 + 15) // 16 * 16)
    moe_s = moe(xs, n_dec, (n_dec // 2 + 15) // 16 * 16)

    wpg = w_ple_gate[1].astype(BF16)
    wpi = w_ple_in[1].astype(BF16)
    xp = _ple(xp, moe_p, _row(g_ple[1]), wpg, p_prompt[1].reshape(T, -1), wpi, tile=tile_p)
    xs = _ple(xs, moe_s, _row(g_ple[1]), wpg, p_sample[1].reshape(n_dec, -1), wpi, tile=n_dec)

    return (xp.reshape(1, T, D), xs.reshape(nb, dec_seq, D), ka_p, va_p, kb_p, vb_out_p,
            ka_s, va_s, kb_s, vb_out_s)
```
